```python
import jax, jax.numpy as jnp
from jax import lax
import numpy as np

D_MODEL = 1024
BATCH = 4
SEQ = 4096
DEPTH = 2

GRID_W = 64
CTX_LEN = 256
N_HEADS = 8
QK_NOPE = 128
QK_ROPE = 64
V_DIM = 128
Q_LORA = 256
KV_LORA = 128
POOL_W = 512
POOL_WINDOWS = (2, 4, 8, 16)
N_POOL_GROUPS = 4
POOL_GROUP = POOL_W // N_POOL_GROUPS
D_FF = -(-8 * D_MODEL // (3 * 256)) * 256
N_BRANCH = 2
N_MOD = 6
ROPE_THETA = 10000.0
ROPE_AXIS = QK_ROPE // 2
EPS = 1e-6
Q_BLOCK = 128
ATTN_SCALE = (QK_NOPE + QK_ROPE) ** -0.5

OFF_KV = Q_LORA
OFF_KR = OFF_KV + KV_LORA
OFF_POOL = OFF_KR + QK_ROPE
OFF_GATE = OFF_POOL + POOL_W
D_IN = OFF_GATE + N_BRANCH * D_MODEL

kernel_name = 'hybrid_mla_pool_dit_block'


def rms_norm(x, g):
    x32 = x.astype(jnp.float32)
    y = x32 * lax.rsqrt(jnp.mean(x32 * x32, axis=-1, keepdims=True) + EPS)
    return (y * g.astype(jnp.float32)).astype(x.dtype)


def modulate(h, shift, scale):
    return h * (1.0 + scale) + shift


def axial_angles(n_tokens):
    rows = n_tokens // GRID_W
    row = jnp.repeat(jnp.arange(rows), GRID_W)
    col = jnp.tile(jnp.arange(GRID_W), rows)
    pos = jnp.stack([row, col], axis=-1).astype(jnp.float32)
    inv_freq = ROPE_THETA ** (-jnp.arange(0, ROPE_AXIS, 2, dtype=jnp.float32) / ROPE_AXIS)
    return pos[:, :, None] * inv_freq


def apply_rope_2d(x, ang):
    shp = x.shape
    xr = x.astype(jnp.float32).reshape(shp[:-1] + (2, 2, QK_ROPE // 4))
    x1, x2 = xr[..., 0, :], xr[..., 1, :]
    cos, sin = jnp.cos(ang), jnp.sin(ang)
    out = jnp.stack([x1 * cos - x2 * sin, x1 * sin + x2 * cos], axis=-2)
    return out.reshape(shp).astype(x.dtype)


def mla_queries(z_q, q_norm, w_uq):
    q = rms_norm(z_q, q_norm) @ w_uq
    q = q.reshape(q.shape[:-1] + (N_HEADS, QK_NOPE + QK_ROPE))
    return q[..., :QK_NOPE], q[..., QK_NOPE:]


def mla_keys_values(z_kv, kv_norm, w_ukv):
    kv = rms_norm(z_kv, kv_norm) @ w_ukv
    kv = kv.reshape(kv.shape[:-1] + (N_HEADS, QK_NOPE + V_DIM))
    return kv[..., :QK_NOPE], kv[..., QK_NOPE:]


def attend(q_nope, q_rope, k_nope, k_rope, v):
    s = jnp.einsum('bqhd,bkhd->bhqk', q_nope, k_nope) + jnp.einsum('bqhr,bkr->bhqk', q_rope, k_rope)
    p = jax.nn.softmax(s.astype(jnp.float32) * ATTN_SCALE, axis=-1).astype(v.dtype)
    return jnp.einsum('bhqk,bkhd->bqhd', p, v)


def attend_blocked(q_nope, q_rope, k_nope, k_rope, v):
    b, t = q_nope.shape[:2]
    nb = t // Q_BLOCK
    qn = q_nope.reshape(b, nb, Q_BLOCK, N_HEADS, QK_NOPE).swapaxes(0, 1)
    qr = q_rope.reshape(b, nb, Q_BLOCK, N_HEADS, QK_ROPE).swapaxes(0, 1)
    o = lax.map(lambda qs: attend(qs[0], qs[1], k_nope, k_rope, v), (qn, qr))
    return o.swapaxes(0, 1).reshape(b, t, N_HEADS, V_DIM)


def multiscale_pool(u, w_pool, pool_scale, w_o_pool):
    b, t = u.shape[:2]
    cs = jnp.pad(jnp.cumsum(u.astype(jnp.float32), axis=1), ((0, 0), (1, 0), (0, 0)))
    pos = jnp.arange(t)
    groups = []
    for g, w in enumerate(POOL_WINDOWS):
        lo = jnp.clip(pos - w // 2, 0, t)
        hi = jnp.clip(pos + (w - w // 2), 0, t)
        sl = slice(g * POOL_GROUP, (g + 1) * POOL_GROUP)
        csg = cs[:, :, sl]
        mean = (csg[:, hi] - csg[:, lo]) / (hi - lo).astype(jnp.float32)[None, :, None]
        groups.append(mean.astype(u.dtype) - u[:, :, sl])
    d = jnp.stack(groups, axis=2)
    y = jnp.einsum('btgc,gce->btge', d, w_pool).reshape(b, t, POOL_W) * pool_scale
    return y @ w_o_pool


def mix_branches(z, o_att, w_o_mla, w_pool, pool_scale, w_o_pool, w_out):
    b, t = z.shape[:2]
    o_pool = multiscale_pool(z[..., OFF_POOL:OFF_GATE], w_pool, pool_scale, w_o_pool)
    gates = jax.nn.sigmoid(z[..., OFF_GATE:]).reshape(b, t, N_BRANCH, D_MODEL)
    merged = gates[:, :, 0] * (o_att.reshape(b, t, N_HEADS * V_DIM) @ w_o_mla) + gates[:, :, 1] * o_pool
    return merged @ w_out


def swiglu(h, w_gate, w_up, w_down):
    return (jax.nn.silu(h @ w_gate) * (h @ w_up)) @ w_down


def setup_inputs(seed: int = 0) -> dict:
    key = jax.random.key(seed)
    ks = jax.random.split(key, 24)
    f32 = jnp.float32

    def nrm(k, shape, fan_in):
        return jax.random.normal(k, shape, f32) * fan_in ** -0.5

    def gain(k, shape):
        return 1.0 + 0.05 * jax.random.normal(k, shape, f32)

    return {
        'x': jax.random.normal(ks[0], (BATCH, SEQ, D_MODEL), f32),
        'c': jax.random.normal(ks[1], (BATCH, D_MODEL), f32),
        'ctx': jax.random.normal(ks[2], (BATCH, CTX_LEN, D_MODEL), f32),
        'c_ctx': jax.random.normal(ks[3], (D_MODEL,), f32),
        'w_ada': nrm(ks[4], (DEPTH, D_MODEL, N_MOD * D_MODEL), D_MODEL),
        'b_ada': 0.01 * jax.random.normal(ks[5], (DEPTH, N_MOD * D_MODEL), f32),
        'g_pre_mix': gain(ks[6], (DEPTH, D_MODEL)),
        'w_in': nrm(ks[7], (DEPTH, D_MODEL, D_IN), D_MODEL),
        'q_norm': gain(ks[8], (DEPTH, Q_LORA)),
        'w_uq': nrm(ks[9], (DEPTH, Q_LORA, N_HEADS * (QK_NOPE + QK_ROPE)), Q_LORA),
        'kv_norm': gain(ks[10], (DEPTH, KV_LORA)),
        'w_ukv': nrm(ks[11], (DEPTH, KV_LORA, N_HEADS * (QK_NOPE + V_DIM)), KV_LORA),
        'w_o_mla': nrm(ks[12], (DEPTH, N_HEADS * V_DIM, D_MODEL), N_HEADS * V_DIM),
        'w_pool': nrm(ks[13], (DEPTH, N_POOL_GROUPS, POOL_GROUP, POOL_GROUP), POOL_GROUP),
        'pool_scale': 1.0 + 0.1 * jax.random.normal(ks[14], (DEPTH, POOL_W), f32),
        'w_o_pool': nrm(ks[15], (DEPTH, POOL_W, D_MODEL), POOL_W),
        'w_out': nrm(ks[16], (DEPTH, D_MODEL, D_MODEL), D_MODEL),
        'g_post_mix': gain(ks[17], (DEPTH, D_MODEL)),
        'g_pre_ffn': gain(ks[18], (DEPTH, D_MODEL)),
        'w_ffn_gate': nrm(ks[19], (DEPTH, D_MODEL, D_FF), D_MODEL),
        'w_ffn_up': nrm(ks[20], (DEPTH, D_MODEL, D_FF), D_MODEL),
        'w_ffn_down': nrm(ks[21], (DEPTH, D_FF, D_MODEL), D_FF),
        'g_post_ffn': gain(ks[22], (DEPTH, D_MODEL)),
    }


def reference(x, c, ctx, c_ctx, w_ada, b_ada, g_pre_mix, w_in, q_norm, w_uq, kv_norm, w_ukv, w_o_mla,
              w_pool, pool_scale, w_o_pool, w_out, g_post_mix, g_pre_ffn, w_ffn_gate, w_ffn_up, w_ffn_down,
              g_post_ffn):
    b, t, _ = x.shape
    ang = axial_angles(t).astype(x.dtype)
    xc = ctx
    for l in range(DEPTH):
        last = l == DEPTH - 1
        mod_x = (jax.nn.silu(c) @ w_ada[l] + b_ada[l]).reshape(b, 1, N_MOD, D_MODEL)
        mod_c = (jax.nn.silu(c_ctx) @ w_ada[l] + b_ada[l]).reshape(1, 1, N_MOD, D_MODEL)

        h = modulate(rms_norm(x, g_pre_mix[l]), mod_x[:, :, 1], mod_x[:, :, 0])
        hc = modulate(rms_norm(xc, g_pre_mix[l]), mod_c[:, :, 1], mod_c[:, :, 0])
        z = h @ w_in[l]
        if last:
            zc_kv = hc @ w_in[l][:, OFF_KV:OFF_POOL]
        else:
            zc = hc @ w_in[l]
            zc_kv = zc[..., OFF_KV:OFF_POOL]
        kc_nope, vc = mla_keys_values(zc_kv[..., :KV_LORA], kv_norm[l], w_ukv[l])
        kc_rope = zc_kv[..., KV_LORA:]
        q_nope, q_rope = mla_queries(z[..., :OFF_KV], q_norm[l], w_uq[l])
        q_rope = apply_rope_2d(q_rope, ang[:, None])
        k_nope, v = mla_keys_values(z[..., OFF_KV:OFF_KR], kv_norm[l], w_ukv[l])
        k_rope = apply_rope_2d(z[..., OFF_KR:OFF_POOL], ang)
        o = attend_blocked(q_nope, q_rope,
                           jnp.concatenate([kc_nope, k_nope], axis=1),
                           jnp.concatenate([kc_rope, k_rope], axis=1),
                           jnp.concatenate([vc, v], axis=1))
        y = mix_branches(z, o, w_o_mla[l], w_pool[l], pool_scale[l], w_o_pool[l], w_out[l])
        x = x + mod_x[:, :, 2] * rms_norm(y, g_post_mix[l])
        if not last:
            qc_nope, qc_rope = mla_queries(zc[..., :OFF_KV], q_norm[l], w_uq[l])
            oc = attend(qc_nope, qc_rope, kc_nope, kc_rope, vc)
            yc = mix_branches(zc, oc, w_o_mla[l], w_pool[l], pool_scale[l], w_o_pool[l], w_out[l])
            xc = xc + mod_c[:, :, 2] * rms_norm(yc, g_post_mix[l])

        hf = modulate(rms_norm(x, g_pre_ffn[l]), mod_x[:, :, 4], mod_x[:, :, 3])
        x = x + mod_x[:, :, 5] * rms_norm(swiglu(hf, w_ffn_gate[l], w_ffn_up[l], w_ffn_down[l]), g_post_ffn[l])
        if not last:
            hfc = modulate(rms_norm(xc, g_pre_ffn[l]), mod_c[:, :, 4], mod_c[:, :, 3])
            xc = xc + mod_c[:, :, 5] * rms_norm(swiglu(hfc, w_ffn_gate[l], w_ffn_up[l], w_ffn_down[l]), g_post_ffn[l])
    return x
```

```python
import functools

import jax
import jax.numpy as jnp
from jax import lax
from jax.experimental import pallas as pl
from jax.experimental.pallas import tpu as pltpu

D_MODEL = 1024
BATCH = 4
SEQ = 4096
DEPTH = 2
GRID_W = 64
CTX_LEN = 256
N_HEADS = 8
QK_NOPE = 128
QK_ROPE = 64
V_DIM = 128
Q_LORA = 256
KV_LORA = 128
POOL_W = 512
POOL_WINDOWS = (2, 4, 8, 16)
N_POOL_GROUPS = 4
POOL_GROUP = POOL_W // N_POOL_GROUPS
D_FF = -(-8 * D_MODEL // (3 * 256)) * 256
N_BRANCH = 2
N_MOD = 6
ROPE_THETA = 10000.0
ROPE_AXIS = QK_ROPE // 2
EPS = 1e-6
ATTN_SCALE = (QK_NOPE + QK_ROPE) ** -0.5

OFF_KV = Q_LORA
OFF_KR = OFF_KV + KV_LORA
OFF_POOL = OFF_KR + QK_ROPE
OFF_GATE = OFF_POOL + POOL_W

F32 = jnp.float32
BF16 = jnp.bfloat16

TM = 256
TS = CTX_LEN + SEQ
NT = TS // TM
NT_LAT = SEQ // TM
HEAD_W = 2 * QK_NOPE
HALO = 8
MOD_ROWS = 8
ADA_BN = 1536
VMEM_LIMIT = 52 * 1024 * 1024

assert CTX_LEN == TM and SEQ % TM == 0 and HALO * 2 == max(POOL_WINDOWS)
assert BATCH + 1 <= MOD_ROWS and (N_MOD * D_MODEL) % ADA_BN == 0


def _const_spec(shape):
    nd = len(shape)
    return pl.BlockSpec(shape, lambda *_: (0,) * nd, pipeline_mode=pl.Buffered(1))


def _rms(x, g):
    return x * lax.rsqrt(jnp.mean(x * x, axis=-1, keepdims=True) + EPS) * g


def _dot(a, b):
    return jnp.dot(a, b, preferred_element_type=F32)


def _ada_kernel(c_ref, w_ref, b_ref, o_ref):
    cv = c_ref[...]
    s = cv * jax.nn.sigmoid(cv)
    o_ref[0] = _dot(s.astype(BF16), w_ref[0].astype(BF16)) + b_ref[0]


def _ada_call(cvec, w_ada, b_ada):
    n_out = N_MOD * D_MODEL
    return pl.pallas_call(
        _ada_kernel,
        grid=(DEPTH, n_out // ADA_BN),
        in_specs=[
            pl.BlockSpec((MOD_ROWS, D_MODEL), lambda l, n: (0, 0)),
            pl.BlockSpec((1, D_MODEL, ADA_BN), lambda l, n: (l, 0, n)),
            pl.BlockSpec((1, 1, ADA_BN), lambda l, n: (l, 0, n)),
        ],
        out_specs=pl.BlockSpec((1, MOD_ROWS, ADA_BN), lambda l, n: (l, 0, n)),
        out_shape=jax.ShapeDtypeStruct((DEPTH, MOD_ROWS, n_out), F32),
        compiler_params=pltpu.CompilerParams(
            dimension_semantics=("arbitrary", "arbitrary"),
            vmem_limit_bytes=VMEM_LIMIT),
        name="ada_mod",
    )(cvec, w_ada, b_ada.reshape(DEPTH, 1, n_out))


def _mod_row(b, j):
    return jnp.where(j == 0, BATCH, b)


def _rope(t, cos, sin):
    return t * cos + pltpu.roll(t, QK_ROPE, axis=1) * sin


def _inproj_kernel(x_ref, mod_ref, gpre_ref, win_ref, qn_ref, wuq_ref, kvn_ref,
                   wukv_ref, cos_ref, sin_ref, q_ref, k_ref, v_ref, u_ref):
    x = x_ref[0]
    m = mod_ref[0]
    h = _rms(x, gpre_ref[...]) * (1.0 + m[0:1]) + m[1:2]
    z = _dot(h.astype(BF16), win_ref[...])
    u_ref[0] = z[:, OFF_POOL + QK_ROPE:]
    cos = cos_ref[...]
    sin = sin_ref[...]

    qn = _rms(z[:, :OFF_KV], qn_ref[...])
    qa = _dot(qn.astype(BF16), wuq_ref[...])
    cn = _rms(z[:, OFF_KV:OFF_KR], kvn_ref[...])
    kv = _dot(cn.astype(BF16), wukv_ref[...])
    kr = _rope(z[:, OFF_KR:OFF_KR + 2 * QK_ROPE], cos, sin).astype(BF16)
    for hd in range(N_HEADS):
        c0 = hd * HEAD_W
        q_ref[0, hd, :, :QK_NOPE] = (qa[:, c0:c0 + QK_NOPE] * ATTN_SCALE).astype(BF16)
        q_ref[0, hd, :, QK_NOPE:] = (
            _rope(qa[:, c0 + QK_NOPE:c0 + HEAD_W], cos, sin) * ATTN_SCALE).astype(BF16)
        k_ref[0, hd, :, :QK_NOPE] = kv[:, c0:c0 + QK_NOPE].astype(BF16)
        k_ref[0, hd, :, QK_NOPE:] = kr
        v_ref[0, hd] = kv[:, c0 + QK_NOPE:c0 + HEAD_W].astype(BF16)


def _inproj_call(xs, mod, gpre, win_a, qnorm, wuq_a, kvnorm, wukv, cos_t, sin_t):
    tile = lambda b, j: (b, j, 0)
    heads = lambda b, j: (b, 0, j, 0)
    return pl.pallas_call(
        _inproj_kernel,
        grid=(BATCH, NT),
        in_specs=[
            pl.BlockSpec((1, TM, D_MODEL), tile),
            pl.BlockSpec((1, N_MOD, D_MODEL), lambda b, j: (_mod_row(b, j), 0, 0)),
            _const_spec((1, D_MODEL)),
            _const_spec(win_a.shape),
            _const_spec((1, Q_LORA)),
            _const_spec(wuq_a.shape),
            _const_spec((1, KV_LORA)),
            _const_spec(wukv.shape),
            pl.BlockSpec((TM, 2 * QK_ROPE), lambda b, j: (j, 0)),
            pl.BlockSpec((TM, 2 * QK_ROPE), lambda b, j: (j, 0)),
        ],
        out_specs=[
            pl.BlockSpec((1, N_HEADS, TM, HEAD_W), heads),
            pl.BlockSpec((1, N_HEADS, TM, HEAD_W), heads),
            pl.BlockSpec((1, N_HEADS, TM, V_DIM), heads),
            pl.BlockSpec((1, TM, POOL_W), tile),
        ],
        out_shape=[
            jax.ShapeDtypeStruct((BATCH, N_HEADS, TS, HEAD_W), BF16),
            jax.ShapeDtypeStruct((BATCH, N_HEADS, TS, HEAD_W), BF16),
            jax.ShapeDtypeStruct((BATCH, N_HEADS, TS, V_DIM), BF16),
            jax.ShapeDtypeStruct((BATCH, TS, POOL_W), F32),
        ],
        compiler_params=pltpu.CompilerParams(
            dimension_semantics=("arbitrary", "arbitrary"),
            vmem_limit_bytes=VMEM_LIMIT),
        name="inproj",
    )(xs, mod, gpre, win_a, qnorm, wuq_a, kvnorm, wukv, cos_t, sin_t)


def _softmax_pv(q, k, v):
    s = lax.dot_general(q, k, (((1,), (1,)), ((), ())), preferred_element_type=F32)
    p = jnp.exp(s - jnp.max(s, axis=-1, keepdims=True))
    l = jnp.sum(p, axis=-1, keepdims=True)
    return _dot(p.astype(BF16), v) / l


def _attn_kernel(q_ref, k_ref, v_ref, o_ref, *, ctx_queries):
    q = q_ref[0, 0]
    if ctx_queries:
        i = pl.program_id(2)

        @pl.when(i == 0)
        def _():
            o_ref[0] = _softmax_pv(q, k_ref[0, 0, :CTX_LEN], v_ref[0, 0, :CTX_LEN]).astype(BF16)

        @pl.when(i != 0)
        def _():
            o_ref[0] = _softmax_pv(q, k_ref[0, 0], v_ref[0, 0]).astype(BF16)
    else:
        o_ref[0] = _softmax_pv(q, k_ref[0, 0], v_ref[0, 0]).astype(BF16)


def _attn_call(q, k, v, ctx_queries):
    n_q = NT if ctx_queries else NT_LAT
    q_off = 0 if ctx_queries else NT - NT_LAT
    return pl.pallas_call(
        functools.partial(_attn_kernel, ctx_queries=ctx_queries),
        grid=(BATCH, N_HEADS, n_q),
        in_specs=[
            pl.BlockSpec((1, 1, TM, HEAD_W), lambda b, h, i: (b, h, i + q_off, 0)),
            pl.BlockSpec((1, 1, TS, HEAD_W), lambda b, h, i: (b, h, 0, 0)),
            pl.BlockSpec((1, 1, TS, V_DIM), lambda b, h, i: (b, h, 0, 0)),
        ],
        out_specs=pl.BlockSpec((1, TM, V_DIM), lambda b, h, i: (b, i, h)),
        out_shape=jax.ShapeDtypeStruct((BATCH, n_q * TM, N_HEADS * V_DIM), BF16),
        compiler_params=pltpu.CompilerParams(
            dimension_semantics=("arbitrary", "arbitrary", "arbitrary"),
            vmem_limit_bytes=VMEM_LIMIT),
        name="attention",
    )(q, k, v)


def _mix_kernel(x_ref, mod_ref, gpre_ref, wg_ref, o_ref, womla_ref, uc_ref, up_ref,
                un_ref, wpool_ref, pscale_ref, wopool_ref, wout_ref, gpost_ref,
                out_ref, ext_ref, *, tile_off):
    j = pl.program_id(1) + tile_off
    x = x_ref[0]
    m = mod_ref[0]
    h = (_rms(x, gpre_ref[...]) * (1.0 + m[0:1]) + m[1:2]).astype(BF16)
    gates = jax.nn.sigmoid(_dot(h, wg_ref[...]))
    att = _dot(o_ref[0], womla_ref[...])

    is_ctx = j == 0
    first = jnp.logical_or(is_ctx, j == NT - NT_LAT)
    last = jnp.logical_or(is_ctx, j == NT - 1)
    u = uc_ref[0]
    ext_ref[0:HALO] = jnp.where(first, 0.0, up_ref[0])
    ext_ref[HALO:HALO + TM] = u
    ext_ref[HALO + TM:] = jnp.where(last, 0.0, un_ref[0])
    seq_len = jnp.where(is_ctx, CTX_LEN, SEQ)
    pos = lax.broadcasted_iota(jnp.int32, (TM, 1), 0) + jnp.where(is_ctx, 0, j - (NT - NT_LAT)) * TM
    ys = []
    for g, w in enumerate(POOL_WINDOWS):
        lanes = slice(g * POOL_GROUP, (g + 1) * POOL_GROUP)
        acc = ext_ref[HALO - w // 2:HALO - w // 2 + TM, lanes]
        for s in range(1 - w // 2, w - w // 2):
            acc = acc + ext_ref[HALO + s:HALO + s + TM, lanes]
        cnt = jnp.minimum(pos + (w - w // 2), seq_len) - jnp.maximum(pos - w // 2, 0)
        d = acc / cnt.astype(F32) - u[:, lanes]
        ys.append(_dot(d.astype(BF16), wpool_ref[g]))
    yp = jnp.concatenate(ys, axis=1) * pscale_ref[...]
    o_pool = _dot(yp.astype(BF16), wopool_ref[...])

    merged = gates[:, :D_MODEL] * att + gates[:, D_MODEL:] * o_pool
    y = _dot(merged.astype(BF16), wout_ref[...])
    out_ref[0] = x + m[2:3] * _rms(y, gpost_ref[...])


def _mix_call(xs, mod, gpre, wg, o, womla, u, wpool, pscale, wopool, wout, gpost, with_ctx):
    n_t = NT if with_ctx else NT_LAT
    off = 0 if with_ctx else NT - NT_LAT
    hb = TM // HALO
    return pl.pallas_call(
        functools.partial(_mix_kernel, tile_off=off),
        grid=(BATCH, n_t),
        in_specs=[
            pl.BlockSpec((1, TM, D_MODEL), lambda b, j: (b, j + off, 0)),
            pl.BlockSpec((1, N_MOD, D_MODEL), lambda b, j: (_mod_row(b, j + off), 0, 0)),
            _const_spec((1, D_MODEL)),
            _const_spec(wg.shape),
            pl.BlockSpec((1, TM, N_HEADS * V_DIM), lambda b, j: (b, j, 0)),
            _const_spec(womla.shape),
            pl.BlockSpec((1, TM, POOL_W), lambda b, j: (b, j + off, 0)),
            pl.BlockSpec((1, HALO, POOL_W),
                         lambda b, j: (b, jnp.maximum((j + off) * hb - 1, 0), 0)),
            pl.BlockSpec((1, HALO, POOL_W),
                         lambda b, j: (b, jnp.minimum((j + off + 1) * hb, TS // HALO - 1), 0)),
            _const_spec(wpool.shape),
            _const_spec((1, POOL_W)),
            _const_spec(wopool.shape),
            _const_spec(wout.shape),
            _const_spec((1, D_MODEL)),
        ],
        out_specs=pl.BlockSpec((1, TM, D_MODEL), lambda b, j: (b, j, 0)),
        out_shape=jax.ShapeDtypeStruct((BATCH, n_t * TM, D_MODEL), F32),
        scratch_shapes=[pltpu.VMEM((TM + 2 * HALO, POOL_W), F32)],
        compiler_params=pltpu.CompilerParams(
            dimension_semantics=("arbitrary", "arbitrary"),
            vmem_limit_bytes=VMEM_LIMIT),
        name="mix",
    )(xs, mod, gpre, wg, o, womla, u, u, u, wpool, pscale, wopool, wout, gpost)


def _ffn_kernel(x_ref, mod_ref, gpre_ref, wgate_ref, wup_ref, wdown_ref, gpost_ref, out_ref):
    x = x_ref[0]
    m = mod_ref[0]
    h = (_rms(x, gpre_ref[...]) * (1.0 + m[3:4]) + m[4:5]).astype(BF16)
    a = _dot(h, wgate_ref[...])
    t = (a * jax.nn.sigmoid(a)) * _dot(h, wup_ref[...])
    y = _dot(t.astype(BF16), wdown_ref[...])
    out_ref[0] = x + m[5:6] * _rms(y, gpost_ref[...])


def _ffn_call(xs, mod, gpre, wgate, wup, wdown, gpost, with_ctx):
    n_t = NT if with_ctx else NT_LAT
    off = 0 if with_ctx else NT - NT_LAT
    tile = lambda b, j: (b, j, 0)
    return pl.pallas_call(
        _ffn_kernel,
        grid=(BATCH, n_t),
        in_specs=[
            pl.BlockSpec((1, TM, D_MODEL), tile),
            pl.BlockSpec((1, N_MOD, D_MODEL), lambda b, j: (_mod_row(b, j + off), 0, 0)),
            _const_spec((1, D_MODEL)),
            _const_spec(wgate.shape),
            _const_spec(wup.shape),
            _const_spec(wdown.shape),
            _const_spec((1, D_MODEL)),
        ],
        out_specs=pl.BlockSpec((1, TM, D_MODEL), tile),
        out_shape=jax.ShapeDtypeStruct((BATCH, n_t * TM, D_MODEL), F32),
        compiler_params=pltpu.CompilerParams(
            dimension_semantics=("arbitrary", "arbitrary"),
            vmem_limit_bytes=VMEM_LIMIT),
        name="ffn",
    )(xs, mod, gpre, wgate, wup, wdown, gpost)


def _rope_tables():
    rows = SEQ // GRID_W
    row = jnp.repeat(jnp.arange(rows), GRID_W)
    col = jnp.tile(jnp.arange(GRID_W), rows)
    pos = jnp.stack([row, col], axis=-1).astype(F32)
    inv_freq = ROPE_THETA ** (-jnp.arange(0, ROPE_AXIS, 2, dtype=F32) / ROPE_AXIS)
    ang = pos[:, :, None] * inv_freq
    cos = jnp.cos(ang)
    sin = jnp.sin(ang)
    cos = jnp.stack([cos, cos], axis=2).reshape(SEQ, QK_ROPE)
    sin = jnp.stack([-sin, sin], axis=2).reshape(SEQ, QK_ROPE)
    cos = jnp.concatenate([jnp.ones((CTX_LEN, QK_ROPE), F32), cos], axis=0)
    sin = jnp.concatenate([jnp.zeros((CTX_LEN, QK_ROPE), F32), sin], axis=0)
    pad = jnp.zeros((TS, QK_ROPE), F32)
    return jnp.concatenate([cos, pad], axis=1), jnp.concatenate([sin, pad], axis=1)


def _rot_cols():
    quarter = QK_ROPE // 4
    idx = []
    for a in range(2):
        base = a * 2 * quarter
        idx += list(range(base + quarter, base + 2 * quarter)) + list(range(base, base + quarter))
    return jnp.asarray(idx, jnp.int32)


def kernel(x, c, ctx, c_ctx, w_ada, b_ada, g_pre_mix, w_in, q_norm, w_uq, kv_norm, w_ukv, w_o_mla,
           w_pool, pool_scale, w_o_pool, w_out, g_post_mix, g_pre_ffn, w_ffn_gate, w_ffn_up,
           w_ffn_down, g_post_ffn):
    assert x.shape == (BATCH, SEQ, D_MODEL) and ctx.shape == (BATCH, CTX_LEN, D_MODEL)
    cvec = jnp.concatenate(
        [c, c_ctx[None, :], jnp.zeros((MOD_ROWS - BATCH - 1, D_MODEL), F32)], axis=0)
    mod_all = _ada_call(cvec, w_ada, b_ada).reshape(DEPTH, MOD_ROWS, N_MOD, D_MODEL)
    cos_t, sin_t = _rope_tables()
    rot = _rot_cols()

    xs = jnp.concatenate([ctx, x], axis=1)
    for l in range(DEPTH):
        last = l == DEPTH - 1
        mod = mod_all[l]
        wl = w_in[l]
        kr_cols = wl[:, OFF_KR:OFF_POOL]
        win_a = jnp.concatenate(
            [wl[:, :OFF_KR], kr_cols, kr_cols[:, rot], wl[:, OFF_POOL:OFF_GATE]], axis=1).astype(BF16)
        wg = wl[:, OFF_GATE:].astype(BF16)
        wq = w_uq[l].reshape(Q_LORA, N_HEADS, QK_NOPE + QK_ROPE)
        wq_rope = wq[:, :, QK_NOPE:]
        wuq_a = jnp.concatenate([wq, wq_rope[:, :, rot]], axis=2).reshape(
            Q_LORA, N_HEADS * HEAD_W).astype(BF16)

        q, k, v, u = _inproj_call(
            xs, mod, g_pre_mix[l][None], win_a, q_norm[l][None], wuq_a, kv_norm[l][None],
            w_ukv[l].astype(BF16), cos_t, sin_t)
        o = _attn_call(q, k, v, ctx_queries=not last)
        xs1 = _mix_call(
            xs, mod, g_pre_mix[l][None], wg, o, w_o_mla[l].astype(BF16), u,
            w_pool[l].astype(BF16), pool_scale[l][None], w_o_pool[l].astype(BF16),
            w_out[l].astype(BF16), g_post_mix[l][None], with_ctx=not last)
        xs = _ffn_call(
            xs1, mod, g_pre_ffn[l][None], w_ffn_gate[l].astype(BF16), w_ffn_up[l].astype(BF16),
            w_ffn_down[l].astype(BF16), g_post_ffn[l][None], with_ctx=not last)
    return xs
```

```python
import functools

import jax
import jax.numpy as jnp
from jax import lax
from jax.experimental import pallas as pl
from jax.experimental.pallas import tpu as pltpu

D_MODEL = 1024
BATCH = 4
SEQ = 4096
DEPTH = 2
GRID_W = 64
CTX_LEN = 256
N_HEADS = 8
QK_NOPE = 128
QK_ROPE = 64
V_DIM = 128
Q_LORA = 256
KV_LORA = 128
POOL_W = 512
POOL_WINDOWS = (2, 4, 8, 16)
N_POOL_GROUPS = 4
POOL_GROUP = POOL_W // N_POOL_GROUPS
D_FF = -(-8 * D_MODEL // (3 * 256)) * 256
N_BRANCH = 2
N_MOD = 6
ROPE_THETA = 10000.0
ROPE_AXIS = QK_ROPE // 2
EPS = 1e-6
ATTN_SCALE = (QK_NOPE + QK_ROPE) ** -0.5

OFF_KV = Q_LORA
OFF_KR = OFF_KV + KV_LORA
OFF_POOL = OFF_KR + QK_ROPE
OFF_GATE = OFF_POOL + POOL_W

F32 = jnp.float32
BF16 = jnp.bfloat16

TM = 256
TS = CTX_LEN + SEQ
NT = TS // TM
NT_LAT = SEQ // TM
HEAD_W = 2 * QK_NOPE
HALO = 8
MOD_ROWS = 8
ADA_BN = 1536
VMEM_LIMIT = 52 * 1024 * 1024

assert CTX_LEN == TM and SEQ % TM == 0 and HALO * 2 == max(POOL_WINDOWS)
assert BATCH + 1 <= MOD_ROWS and (N_MOD * D_MODEL) % ADA_BN == 0


def _const_spec(shape):
    nd = len(shape)
    return pl.BlockSpec(shape, lambda *_: (0,) * nd, pipeline_mode=pl.Buffered(1))


def _rms(x, g):
    return x * lax.rsqrt(jnp.mean(x * x, axis=-1, keepdims=True) + EPS) * g


def _dot(a, b):
    return jnp.dot(a, b, preferred_element_type=F32)


def _ada_kernel(c_ref, w_ref, b_ref, o_ref):
    cv = c_ref[...]
    s = cv * jax.nn.sigmoid(cv)
    o_ref[0] = _dot(s.astype(BF16), w_ref[0].astype(BF16)) + b_ref[0]


def _ada_call(cvec, w_ada, b_ada):
    n_out = N_MOD * D_MODEL
    return pl.pallas_call(
        _ada_kernel,
        grid=(DEPTH, n_out // ADA_BN),
        in_specs=[
            pl.BlockSpec((MOD_ROWS, D_MODEL), lambda l, n: (0, 0)),
            pl.BlockSpec((1, D_MODEL, ADA_BN), lambda l, n: (l, 0, n)),
            pl.BlockSpec((1, 1, ADA_BN), lambda l, n: (l, 0, n)),
        ],
        out_specs=pl.BlockSpec((1, MOD_ROWS, ADA_BN), lambda l, n: (l, 0, n)),
        out_shape=jax.ShapeDtypeStruct((DEPTH, MOD_ROWS, n_out), F32),
        compiler_params=pltpu.CompilerParams(
            dimension_semantics=("arbitrary", "arbitrary"),
            vmem_limit_bytes=VMEM_LIMIT),
        name="ada_mod",
    )(cvec, w_ada, b_ada.reshape(DEPTH, 1, n_out))


def _mod_row(b, j):
    return jnp.where(j == 0, BATCH, b)


def _rope(t, cos, sin):
    return t * cos + pltpu.roll(t, QK_ROPE, axis=1) * sin


def _inproj_kernel(x_ref, mod_ref, gpre_ref, win_ref, qn_ref, wuq_ref, kvn_ref,
                   wuk_ref, wuvt_ref, cos_ref, sin_ref, q_ref, k_ref, vt_ref, u_ref):
    x = x_ref[0]
    m = mod_ref[0]
    h = _rms(x, gpre_ref[...]) * (1.0 + m[0:1]) + m[1:2]
    z = _dot(h.astype(BF16), win_ref[...])
    u_ref[0] = z[:, OFF_POOL + QK_ROPE:]
    cos = cos_ref[...]
    sin = sin_ref[...]

    qn = _rms(z[:, :OFF_KV], qn_ref[...])
    qa = _dot(qn.astype(BF16), wuq_ref[...])
    cn = _rms(z[:, OFF_KV:OFF_KR], kvn_ref[...])
    kn = _dot(cn.astype(BF16), wuk_ref[...])
    vt = _dot(wuvt_ref[...], cn.T.astype(BF16))
    kr = _rope(z[:, OFF_KR:OFF_KR + 2 * QK_ROPE], cos, sin).astype(BF16)
    for hd in range(N_HEADS):
        c0 = hd * HEAD_W
        q_ref[0, hd, :, :QK_NOPE] = (qa[:, c0:c0 + QK_NOPE] * ATTN_SCALE).astype(BF16)
        q_ref[0, hd, :, QK_NOPE:] = (
            _rope(qa[:, c0 + QK_NOPE:c0 + HEAD_W], cos, sin) * ATTN_SCALE).astype(BF16)
        k_ref[0, hd, :, :QK_NOPE] = kn[:, hd * QK_NOPE:(hd + 1) * QK_NOPE].astype(BF16)
        k_ref[0, hd, :, QK_NOPE:] = kr
        vt_ref[0, hd] = vt[hd * V_DIM:(hd + 1) * V_DIM].astype(BF16)


def _inproj_call(xs, mod, gpre, win_a, qnorm, wuq_a, kvnorm, wuk, wuvt, cos_t, sin_t):
    tile = lambda b, j: (b, j, 0)
    heads = lambda b, j: (b, 0, j, 0)
    return pl.pallas_call(
        _inproj_kernel,
        grid=(BATCH, NT),
        in_specs=[
            pl.BlockSpec((1, TM, D_MODEL), tile),
            pl.BlockSpec((1, N_MOD, D_MODEL), lambda b, j: (_mod_row(b, j), 0, 0)),
            _const_spec((1, D_MODEL)),
            _const_spec(win_a.shape),
            _const_spec((1, Q_LORA)),
            _const_spec(wuq_a.shape),
            _const_spec((1, KV_LORA)),
            _const_spec(wuk.shape),
            _const_spec(wuvt.shape),
            pl.BlockSpec((TM, 2 * QK_ROPE), lambda b, j: (j, 0)),
            pl.BlockSpec((TM, 2 * QK_ROPE), lambda b, j: (j, 0)),
        ],
        out_specs=[
            pl.BlockSpec((1, N_HEADS, TM, HEAD_W), heads),
            pl.BlockSpec((1, N_HEADS, TM, HEAD_W), heads),
            pl.BlockSpec((1, N_HEADS, V_DIM, TM), lambda b, j: (b, 0, 0, j)),
            pl.BlockSpec((1, TM, POOL_W), tile),
        ],
        out_shape=[
            jax.ShapeDtypeStruct((BATCH, N_HEADS, TS, HEAD_W), BF16),
            jax.ShapeDtypeStruct((BATCH, N_HEADS, TS, HEAD_W), BF16),
            jax.ShapeDtypeStruct((BATCH, N_HEADS, V_DIM, TS), BF16),
            jax.ShapeDtypeStruct((BATCH, TS, POOL_W), F32),
        ],
        compiler_params=pltpu.CompilerParams(
            dimension_semantics=("arbitrary", "arbitrary"),
            vmem_limit_bytes=VMEM_LIMIT),
        name="inproj",
    )(xs, mod, gpre, win_a, qnorm, wuq_a, kvnorm, wuk, wuvt, cos_t, sin_t)


def _scores_t(k, q):
    return lax.dot_general(k, q, (((1,), (1,)), ((), ())), preferred_element_type=F32)


def _softmax_pv_t(s, m, vt):
    p = jnp.exp(s - m)
    l = jnp.sum(p, axis=0, keepdims=True)
    return (_dot(vt, p.astype(BF16)) / l).T


def _attn_kernel(q_ref, k_ref, vt_ref, o_ref, *rest, ctx_queries):
    if ctx_queries:
        octx_ref, sa_ref, sb_ref, ma_ref, mb_ref = rest
    else:
        sa_ref, sb_ref, ma_ref, mb_ref = rest
    i = pl.program_id(2)

    def scores_stage(t, s_ref, m_ref):
        r = pl.multiple_of((t + (NT - NT_LAT)) * TM, TM)
        s = _scores_t(k_ref[0, 0], q_ref[0, 0, pl.ds(r, TM), :])
        s_ref[...] = s
        m_ref[...] = jnp.max(s, axis=0, keepdims=True)

    def output_stage(s_ref, m_ref, half):
        o_ref[0, half * TM:(half + 1) * TM, :] = _softmax_pv_t(
            s_ref[...], m_ref[...], vt_ref[0, 0]).astype(BF16)

    @pl.when(i == 0)
    def _():
        scores_stage(0, sa_ref, ma_ref)
        if ctx_queries:
            sc = _scores_t(k_ref[0, 0, :CTX_LEN], q_ref[0, 0, :CTX_LEN])
            octx_ref[0] = _softmax_pv_t(
                sc, jnp.max(sc, axis=0, keepdims=True), vt_ref[0, 0, :, :CTX_LEN]).astype(BF16)

    scores_stage(2 * i + 1, sb_ref, mb_ref)
    output_stage(sa_ref, ma_ref, 0)
    scores_stage(jnp.minimum(2 * i + 2, NT_LAT - 1), sa_ref, ma_ref)
    output_stage(sb_ref, mb_ref, 1)


def _attn_call(q, k, vt, ctx_queries):
    head = lambda b, h, i: (b, h, 0, 0)
    out_specs = [pl.BlockSpec((1, 2 * TM, V_DIM), lambda b, h, i: (b, i, h))]
    out_shape = [jax.ShapeDtypeStruct((BATCH, SEQ, N_HEADS * V_DIM), BF16)]
    if ctx_queries:
        out_specs.append(pl.BlockSpec((1, CTX_LEN, V_DIM), lambda b, h, i: (b, 0, h)))
        out_shape.append(jax.ShapeDtypeStruct((BATCH, CTX_LEN, N_HEADS * V_DIM), BF16))
    return pl.pallas_call(
        functools.partial(_attn_kernel, ctx_queries=ctx_queries),
        grid=(BATCH, N_HEADS, NT_LAT // 2),
        in_specs=[
            pl.BlockSpec((1, 1, TS, HEAD_W), head),
            pl.BlockSpec((1, 1, TS, HEAD_W), head),
            pl.BlockSpec((1, 1, V_DIM, TS), head),
        ],
        out_specs=out_specs,
        out_shape=out_shape,
        scratch_shapes=[pltpu.VMEM((TS, TM), F32), pltpu.VMEM((TS, TM), F32),
                        pltpu.VMEM((1, TM), F32), pltpu.VMEM((1, TM), F32)],
        compiler_params=pltpu.CompilerParams(
            dimension_semantics=("arbitrary", "arbitrary", "arbitrary"),
            vmem_limit_bytes=VMEM_LIMIT),
        name="attention",
    )(q, k, vt)


def _mix_kernel(x_ref, mod_ref, gpre_ref, wg_ref, o_ref, *rest, tile_off, with_ctx):
    if with_ctx:
        octx_ref, *rest = rest
    (womla_ref, uc_ref, up_ref, un_ref, wpool_ref, pscale_ref, wopool_ref, wout_ref,
     gpost_ref, out_ref, ext_ref) = rest
    j = pl.program_id(1) + tile_off
    x = x_ref[0]
    m = mod_ref[0]
    h = (_rms(x, gpre_ref[...]) * (1.0 + m[0:1]) + m[1:2]).astype(BF16)
    gates = jax.nn.sigmoid(_dot(h, wg_ref[...]))
    o = jnp.where(j == 0, octx_ref[0], o_ref[0]) if with_ctx else o_ref[0]
    att = _dot(o, womla_ref[...])

    is_ctx = j == 0
    first = jnp.logical_or(is_ctx, j == NT - NT_LAT)
    last = jnp.logical_or(is_ctx, j == NT - 1)
    u = uc_ref[0]
    ext_ref[0:HALO] = jnp.where(first, 0.0, up_ref[0])
    ext_ref[HALO:HALO + TM] = u
    ext_ref[HALO + TM:] = jnp.where(last, 0.0, un_ref[0])
    seq_len = jnp.where(is_ctx, CTX_LEN, SEQ)
    pos = lax.broadcasted_iota(jnp.int32, (TM, 1), 0) + jnp.where(is_ctx, 0, j - (NT - NT_LAT)) * TM
    ys = []
    for g, w in enumerate(POOL_WINDOWS):
        lanes = slice(g * POOL_GROUP, (g + 1) * POOL_GROUP)
        acc = ext_ref[HALO - w // 2:HALO - w // 2 + TM, lanes]
        for s in range(1 - w // 2, w - w // 2):
            acc = acc + ext_ref[HALO + s:HALO + s + TM, lanes]
        cnt = jnp.minimum(pos + (w - w // 2), seq_len) - jnp.maximum(pos - w // 2, 0)
        d = acc / cnt.astype(F32) - u[:, lanes]
        ys.append(_dot(d.astype(BF16), wpool_ref[g]))
    yp = jnp.concatenate(ys, axis=1) * pscale_ref[...]
    o_pool = _dot(yp.astype(BF16), wopool_ref[...])

    merged = gates[:, :D_MODEL] * att + gates[:, D_MODEL:] * o_pool
    y = _dot(merged.astype(BF16), wout_ref[...])
    out_ref[0] = x + m[2:3] * _rms(y, gpost_ref[...])


def _mix_call(xs, mod, gpre, wg, o, o_ctx, womla, u, wpool, pscale, wopool, wout, gpost):
    with_ctx = o_ctx is not None
    n_t = NT if with_ctx else NT_LAT
    off = 0 if with_ctx else NT - NT_LAT
    lat0 = NT - NT_LAT
    hb = TM // HALO
    o_specs = [pl.BlockSpec((1, TM, N_HEADS * V_DIM),
                            lambda b, j: (b, jnp.maximum(j + off - lat0, 0), 0))]
    o_args = [o]
    if with_ctx:
        o_specs.append(pl.BlockSpec((1, CTX_LEN, N_HEADS * V_DIM), lambda b, j: (b, 0, 0)))
        o_args.append(o_ctx)
    return pl.pallas_call(
        functools.partial(_mix_kernel, tile_off=off, with_ctx=with_ctx),
        grid=(BATCH, n_t),
        in_specs=[
            pl.BlockSpec((1, TM, D_MODEL), lambda b, j: (b, j + off, 0)),
            pl.BlockSpec((1, N_MOD, D_MODEL), lambda b, j: (_mod_row(b, j + off), 0, 0)),
            _const_spec((1, D_MODEL)),
            _const_spec(wg.shape),
            *o_specs,
            _const_spec(womla.shape),
            pl.BlockSpec((1, TM, POOL_W), lambda b, j: (b, j + off, 0)),
            pl.BlockSpec((1, HALO, POOL_W),
                         lambda b, j: (b, jnp.maximum((j + off) * hb - 1, 0), 0)),
            pl.BlockSpec((1, HALO, POOL_W),
                         lambda b, j: (b, jnp.minimum((j + off + 1) * hb, TS // HALO - 1), 0)),
            _const_spec(wpool.shape),
            _const_spec((1, POOL_W)),
            _const_spec(wopool.shape),
            _const_spec(wout.shape),
            _const_spec((1, D_MODEL)),
        ],
        out_specs=pl.BlockSpec((1, TM, D_MODEL), lambda b, j: (b, j, 0)),
        out_shape=jax.ShapeDtypeStruct((BATCH, n_t * TM, D_MODEL), F32),
        scratch_shapes=[pltpu.VMEM((TM + 2 * HALO, POOL_W), F32)],
        compiler_params=pltpu.CompilerParams(
            dimension_semantics=("arbitrary", "arbitrary"),
            vmem_limit_bytes=VMEM_LIMIT),
        name="mix",
    )(xs, mod, gpre, wg, *o_args, womla, u, u, u, wpool, pscale, wopool, wout, gpost)


def _ffn_kernel(x_ref, mod_ref, gpre_ref, wgate_ref, wup_ref, wdown_ref, gpost_ref, out_ref):
    x = x_ref[0]
    m = mod_ref[0]
    h = (_rms(x, gpre_ref[...]) * (1.0 + m[3:4]) + m[4:5]).astype(BF16)
    a = _dot(h, wgate_ref[...])
    t = (a * jax.nn.sigmoid(a)) * _dot(h, wup_ref[...])
    y = _dot(t.astype(BF16), wdown_ref[...])
    out_ref[0] = x + m[5:6] * _rms(y, gpost_ref[...])


def _ffn_call(xs, mod, gpre, wgate, wup, wdown, gpost, with_ctx):
    n_t = NT if with_ctx else NT_LAT
    off = 0 if with_ctx else NT - NT_LAT
    tile = lambda b, j: (b, j, 0)
    return pl.pallas_call(
        _ffn_kernel,
        grid=(BATCH, n_t),
        in_specs=[
            pl.BlockSpec((1, TM, D_MODEL), tile),
            pl.BlockSpec((1, N_MOD, D_MODEL), lambda b, j: (_mod_row(b, j + off), 0, 0)),
            _const_spec((1, D_MODEL)),
            _const_spec(wgate.shape),
            _const_spec(wup.shape),
            _const_spec(wdown.shape),
            _const_spec((1, D_MODEL)),
        ],
        out_specs=pl.BlockSpec((1, TM, D_MODEL), tile),
        out_shape=jax.ShapeDtypeStruct((BATCH, n_t * TM, D_MODEL), F32),
        compiler_params=pltpu.CompilerParams(
            dimension_semantics=("arbitrary", "arbitrary"),
            vmem_limit_bytes=VMEM_LIMIT),
        name="ffn",
    )(xs, mod, gpre, wgate, wup, wdown, gpost)


def _rope_tables():
    rows = SEQ // GRID_W
    row = jnp.repeat(jnp.arange(rows), GRID_W)
    col = jnp.tile(jnp.arange(GRID_W), rows)
    pos = jnp.stack([row, col], axis=-1).astype(F32)
    inv_freq = ROPE_THETA ** (-jnp.arange(0, ROPE_AXIS, 2, dtype=F32) / ROPE_AXIS)
    ang = pos[:, :, None] * inv_freq
    cos = jnp.cos(ang)
    sin = jnp.sin(ang)
    cos = jnp.stack([cos, cos], axis=2).reshape(SEQ, QK_ROPE)
    sin = jnp.stack([-sin, sin], axis=2).reshape(SEQ, QK_ROPE)
    cos = jnp.concatenate([jnp.ones((CTX_LEN, QK_ROPE), F32), cos], axis=0)
    sin = jnp.concatenate([jnp.zeros((CTX_LEN, QK_ROPE), F32), sin], axis=0)
    pad = jnp.zeros((TS, QK_ROPE), F32)
    return jnp.concatenate([cos, pad], axis=1), jnp.concatenate([sin, pad], axis=1)


def _rot_cols():
    quarter = QK_ROPE // 4
    idx = []
    for a in range(2):
        base = a * 2 * quarter
        idx += list(range(base + quarter, base + 2 * quarter)) + list(range(base, base + quarter))
    return jnp.asarray(idx, jnp.int32)


def kernel(x, c, ctx, c_ctx, w_ada, b_ada, g_pre_mix, w_in, q_norm, w_uq, kv_norm, w_ukv, w_o_mla,
           w_pool, pool_scale, w_o_pool, w_out, g_post_mix, g_pre_ffn, w_ffn_gate, w_ffn_up,
           w_ffn_down, g_post_ffn):
    assert x.shape == (BATCH, SEQ, D_MODEL) and ctx.shape == (BATCH, CTX_LEN, D_MODEL)
    cvec = jnp.concatenate(
        [c, c_ctx[None, :], jnp.zeros((MOD_ROWS - BATCH - 1, D_MODEL), F32)], axis=0)
    mod_all = _ada_call(cvec, w_ada, b_ada).reshape(DEPTH, MOD_ROWS, N_MOD, D_MODEL)
    cos_t, sin_t = _rope_tables()
    rot = _rot_cols()

    xs = jnp.concatenate([ctx, x], axis=1)
    for l in range(DEPTH):
        last = l == DEPTH - 1
        mod = mod_all[l]
        wl = w_in[l]
        kr_cols = wl[:, OFF_KR:OFF_POOL]
        win_a = jnp.concatenate(
            [wl[:, :OFF_KR], kr_cols, kr_cols[:, rot], wl[:, OFF_POOL:OFF_GATE]], axis=1).astype(BF16)
        wg = wl[:, OFF_GATE:].astype(BF16)
        wq = w_uq[l].reshape(Q_LORA, N_HEADS, QK_NOPE + QK_ROPE)
        wq_rope = wq[:, :, QK_NOPE:]
        wuq_a = jnp.concatenate([wq, wq_rope[:, :, rot]], axis=2).reshape(
            Q_LORA, N_HEADS * HEAD_W).astype(BF16)

        wkv = w_ukv[l].reshape(KV_LORA, N_HEADS, QK_NOPE + V_DIM)
        wuk = wkv[:, :, :QK_NOPE].reshape(KV_LORA, N_HEADS * QK_NOPE).astype(BF16)
        wuvt = wkv[:, :, QK_NOPE:].reshape(KV_LORA, N_HEADS * V_DIM).T.astype(BF16)

        q, k, vt, u = _inproj_call(
            xs, mod, g_pre_mix[l][None], win_a, q_norm[l][None], wuq_a, kv_norm[l][None],
            wuk, wuvt, cos_t, sin_t)
        if last:
            (o,), o_ctx = _attn_call(q, k, vt, ctx_queries=False), None
        else:
            o, o_ctx = _attn_call(q, k, vt, ctx_queries=True)
        xs1 = _mix_call(
            xs, mod, g_pre_mix[l][None], wg, o, o_ctx, w_o_mla[l].astype(BF16), u,
            w_pool[l].astype(BF16), pool_scale[l][None], w_o_pool[l].astype(BF16),
            w_out[l].astype(BF16), g_post_mix[l][None])
        xs = _ffn_call(
            xs1, mod, g_pre_ffn[l][None], w_ffn_gate[l].astype(BF16), w_ffn_up[l].astype(BF16),
            w_ffn_down[l].astype(BF16), g_post_ffn[l][None], with_ctx=not last)
    return xs
```

```python
import functools

import jax
import jax.numpy as jnp
from jax import lax
from jax.experimental import pallas as pl
from jax.experimental.pallas import tpu as pltpu

D_MODEL = 1024
BATCH = 4
SEQ = 4096
DEPTH = 2
GRID_W = 64
CTX_LEN = 256
N_HEADS = 8
QK_NOPE = 128
QK_ROPE = 64
V_DIM = 128
Q_LORA = 256
KV_LORA = 128
POOL_W = 512
POOL_WINDOWS = (2, 4, 8, 16)
N_POOL_GROUPS = 4
POOL_GROUP = POOL_W // N_POOL_GROUPS
D_FF = -(-8 * D_MODEL // (3 * 256)) * 256
N_BRANCH = 2
N_MOD = 6
ROPE_THETA = 10000.0
ROPE_AXIS = QK_ROPE // 2
EPS = 1e-6
ATTN_SCALE = (QK_NOPE + QK_ROPE) ** -0.5
Q_SCALE = ATTN_SCALE * 1.4426950408889634

OFF_KV = Q_LORA
OFF_KR = OFF_KV + KV_LORA
OFF_POOL = OFF_KR + QK_ROPE
OFF_GATE = OFF_POOL + POOL_W

F32 = jnp.float32
BF16 = jnp.bfloat16

TM = 256
TS = CTX_LEN + SEQ
NT = TS // TM
NT_LAT = SEQ // TM
HEAD_W = 2 * QK_NOPE
VT_ROWS = V_DIM + 16
HALO = 8
MOD_ROWS = 8
ADA_BN = 1536
VMEM_LIMIT = 52 * 1024 * 1024

assert CTX_LEN == TM and SEQ % TM == 0 and HALO * 2 == max(POOL_WINDOWS)
assert BATCH + 1 <= MOD_ROWS and (N_MOD * D_MODEL) % ADA_BN == 0


def _const_spec(shape):
    nd = len(shape)
    return pl.BlockSpec(shape, lambda *_: (0,) * nd, pipeline_mode=pl.Buffered(1))


def _rms(x, g):
    return x * lax.rsqrt(jnp.mean(x * x, axis=-1, keepdims=True) + EPS) * g


def _dot(a, b):
    return jnp.dot(a, b, preferred_element_type=F32)


def _ada_kernel(c_ref, w_ref, b_ref, o_ref):
    cv = c_ref[...]
    s = cv * jax.nn.sigmoid(cv)
    o_ref[0] = _dot(s.astype(BF16), w_ref[0].astype(BF16)) + b_ref[0]


def _ada_call(cvec, w_ada, b_ada):
    n_out = N_MOD * D_MODEL
    return pl.pallas_call(
        _ada_kernel,
        grid=(DEPTH, n_out // ADA_BN),
        in_specs=[
            pl.BlockSpec((MOD_ROWS, D_MODEL), lambda l, n: (0, 0)),
            pl.BlockSpec((1, D_MODEL, ADA_BN), lambda l, n: (l, 0, n)),
            pl.BlockSpec((1, 1, ADA_BN), lambda l, n: (l, 0, n)),
        ],
        out_specs=pl.BlockSpec((1, MOD_ROWS, ADA_BN), lambda l, n: (l, 0, n)),
        out_shape=jax.ShapeDtypeStruct((DEPTH, MOD_ROWS, n_out), F32),
        compiler_params=pltpu.CompilerParams(
            dimension_semantics=("arbitrary", "arbitrary"),
            vmem_limit_bytes=VMEM_LIMIT),
        name="ada_mod",
    )(cvec, w_ada, b_ada.reshape(DEPTH, 1, n_out))


def _mod_row(b, j):
    return jnp.where(j == 0, BATCH, b)


def _rope(t, cos, sin):
    return t * cos + pltpu.roll(t, QK_ROPE, axis=1) * sin


def _inproj_kernel(x_ref, mod_ref, gpre_ref, win_ref, qn_ref, wuq_ref, kvn_ref,
                   wuk_ref, wuvt_ref, cos_ref, sin_ref, q_ref, k_ref, vt_ref, u_ref):
    x = x_ref[0]
    m = mod_ref[0]
    h = _rms(x, gpre_ref[...]) * (1.0 + m[0:1]) + m[1:2]
    z = _dot(h.astype(BF16), win_ref[...])
    u_ref[0] = z[:, OFF_POOL + QK_ROPE:]
    cos = cos_ref[...]
    sin = sin_ref[...]

    qn = _rms(z[:, :OFF_KV], qn_ref[...])
    qa = _dot(qn.astype(BF16), wuq_ref[...])
    cn = _rms(z[:, OFF_KV:OFF_KR], kvn_ref[...])
    kn = _dot(cn.astype(BF16), wuk_ref[...])
    vt = _dot(wuvt_ref[...], cn.T.astype(BF16))
    kr = _rope(z[:, OFF_KR:OFF_KR + 2 * QK_ROPE], cos, sin).astype(BF16)
    ones_row = (lax.broadcasted_iota(jnp.int32, (VT_ROWS - V_DIM, TM), 0) == 0).astype(BF16)
    for hd in range(N_HEADS):
        c0 = hd * HEAD_W
        q_ref[0, hd, :, :QK_NOPE] = (qa[:, c0:c0 + QK_NOPE] * Q_SCALE).astype(BF16)
        q_ref[0, hd, :, QK_NOPE:] = (
            _rope(qa[:, c0 + QK_NOPE:c0 + HEAD_W], cos, sin) * Q_SCALE).astype(BF16)
        k_ref[0, hd, :, :QK_NOPE] = kn[:, hd * QK_NOPE:(hd + 1) * QK_NOPE].astype(BF16)
        k_ref[0, hd, :, QK_NOPE:] = kr
        vt_ref[0, hd, :V_DIM] = vt[hd * V_DIM:(hd + 1) * V_DIM].astype(BF16)
        vt_ref[0, hd, V_DIM:] = ones_row


def _inproj_call(xs, mod, gpre, win_a, qnorm, wuq_a, kvnorm, wuk, wuvt, cos_t, sin_t):
    tile = lambda b, j: (b, j, 0)
    heads = lambda b, j: (b, 0, j, 0)
    return pl.pallas_call(
        _inproj_kernel,
        grid=(BATCH, NT),
        in_specs=[
            pl.BlockSpec((1, TM, D_MODEL), tile),
            pl.BlockSpec((1, N_MOD, D_MODEL), lambda b, j: (_mod_row(b, j), 0, 0)),
            _const_spec((1, D_MODEL)),
            _const_spec(win_a.shape),
            _const_spec((1, Q_LORA)),
            _const_spec(wuq_a.shape),
            _const_spec((1, KV_LORA)),
            _const_spec(wuk.shape),
            _const_spec(wuvt.shape),
            pl.BlockSpec((TM, 2 * QK_ROPE), lambda b, j: (j, 0)),
            pl.BlockSpec((TM, 2 * QK_ROPE), lambda b, j: (j, 0)),
        ],
        out_specs=[
            pl.BlockSpec((1, N_HEADS, TM, HEAD_W), heads),
            pl.BlockSpec((1, N_HEADS, TM, HEAD_W), heads),
            pl.BlockSpec((1, N_HEADS, VT_ROWS, TM), lambda b, j: (b, 0, 0, j)),
            pl.BlockSpec((1, TM, POOL_W), tile),
        ],
        out_shape=[
            jax.ShapeDtypeStruct((BATCH, N_HEADS, TS, HEAD_W), BF16),
            jax.ShapeDtypeStruct((BATCH, N_HEADS, TS, HEAD_W), BF16),
            jax.ShapeDtypeStruct((BATCH, N_HEADS, VT_ROWS, TS), BF16),
            jax.ShapeDtypeStruct((BATCH, TS, POOL_W), F32),
        ],
        compiler_params=pltpu.CompilerParams(
            dimension_semantics=("arbitrary", "arbitrary"),
            vmem_limit_bytes=VMEM_LIMIT),
        name="inproj",
    )(xs, mod, gpre, win_a, qnorm, wuq_a, kvnorm, wuk, wuvt, cos_t, sin_t)


def _scores_t(k, q):
    return lax.dot_general(k, q, (((1,), (1,)), ((), ())), preferred_element_type=F32)


MAX_ACC_ROWS = 64


def _col_max(s):
    m = s[:MAX_ACC_ROWS]
    for r in range(MAX_ACC_ROWS, s.shape[0], MAX_ACC_ROWS):
        m = jnp.maximum(m, s[r:r + MAX_ACC_ROWS])
    return jnp.max(m, axis=0, keepdims=True)


def _softmax_pv_t(s, m, vt):
    p = jnp.exp2(s - m)
    ot = _dot(vt, p.astype(BF16))
    return (ot[:V_DIM] / ot[V_DIM:V_DIM + 1]).T


def _attn_kernel(q_ref, k_ref, vt_ref, o_ref, *rest, ctx_queries):
    if ctx_queries:
        octx_ref, sa_ref, sb_ref, ma_ref, mb_ref = rest
    else:
        sa_ref, sb_ref, ma_ref, mb_ref = rest
    i = pl.program_id(2)

    def scores_stage(t, s_ref, m_ref):
        r = pl.multiple_of((t + (NT - NT_LAT)) * TM, TM)
        s = _scores_t(k_ref[0, 0], q_ref[0, 0, pl.ds(r, TM), :])
        s_ref[...] = s
        m_ref[...] = _col_max(s)

    def output_stage(s_ref, m_ref):
        o_ref[0] = _softmax_pv_t(s_ref[...], m_ref[...], vt_ref[0, 0]).astype(BF16)

    @pl.when(i == 0)
    def _():
        scores_stage(0, sa_ref, ma_ref)
        if ctx_queries:
            sc = _scores_t(k_ref[0, 0, :CTX_LEN], q_ref[0, 0, :CTX_LEN])
            octx_ref[0] = _softmax_pv_t(
                sc, _col_max(sc), vt_ref[0, 0, :, :CTX_LEN]).astype(BF16)

    nxt = jnp.minimum(i + 1, NT_LAT - 1)

    @pl.when(i % 2 == 0)
    def _():
        scores_stage(nxt, sb_ref, mb_ref)
        output_stage(sa_ref, ma_ref)

    @pl.when(i % 2 == 1)
    def _():
        scores_stage(nxt, sa_ref, ma_ref)
        output_stage(sb_ref, mb_ref)


def _attn_call(q, k, vt, ctx_queries):
    head = lambda b, h, i: (b, h, 0, 0)
    out_specs = [pl.BlockSpec((1, TM, V_DIM), lambda b, h, i: (b, i, h))]
    out_shape = [jax.ShapeDtypeStruct((BATCH, SEQ, N_HEADS * V_DIM), BF16)]
    if ctx_queries:
        out_specs.append(pl.BlockSpec((1, CTX_LEN, V_DIM), lambda b, h, i: (b, 0, h)))
        out_shape.append(jax.ShapeDtypeStruct((BATCH, CTX_LEN, N_HEADS * V_DIM), BF16))
    return pl.pallas_call(
        functools.partial(_attn_kernel, ctx_queries=ctx_queries),
        grid=(BATCH, N_HEADS, NT_LAT),
        in_specs=[
            pl.BlockSpec((1, 1, TS, HEAD_W), head),
            pl.BlockSpec((1, 1, TS, HEAD_W), head),
            pl.BlockSpec((1, 1, VT_ROWS, TS), head),
        ],
        out_specs=out_specs,
        out_shape=out_shape,
        scratch_shapes=[pltpu.VMEM((TS, TM), F32), pltpu.VMEM((TS, TM), F32),
                        pltpu.VMEM((1, TM), F32), pltpu.VMEM((1, TM), F32)],
        compiler_params=pltpu.CompilerParams(
            dimension_semantics=("arbitrary", "arbitrary", "arbitrary"),
            vmem_limit_bytes=VMEM_LIMIT),
        name="attention",
    )(q, k, vt)


def _mix_kernel(x_ref, mod_ref, gpre_ref, wg_ref, o_ref, *rest, tile_off, with_ctx):
    if with_ctx:
        octx_ref, *rest = rest
    (womla_ref, uc_ref, up_ref, un_ref, wpool_ref, pscale_ref, wopool_ref, wout_ref,
     gpost_ref, out_ref, ext_ref) = rest
    j = pl.program_id(1) + tile_off
    x = x_ref[0]
    m = mod_ref[0]
    h = (_rms(x, gpre_ref[...]) * (1.0 + m[0:1]) + m[1:2]).astype(BF16)
    gates = jax.nn.sigmoid(_dot(h, wg_ref[...]))
    o = jnp.where(j == 0, octx_ref[0], o_ref[0]) if with_ctx else o_ref[0]
    att = _dot(o, womla_ref[...])

    is_ctx = j == 0
    first = jnp.logical_or(is_ctx, j == NT - NT_LAT)
    last = jnp.logical_or(is_ctx, j == NT - 1)
    u = uc_ref[0]
    ext_ref[0:HALO] = jnp.where(first, 0.0, up_ref[0])
    ext_ref[HALO:HALO + TM] = u
    ext_ref[HALO + TM:] = jnp.where(last, 0.0, un_ref[0])
    seq_len = jnp.where(is_ctx, CTX_LEN, SEQ)
    pos = lax.broadcasted_iota(jnp.int32, (TM, 1), 0) + jnp.where(is_ctx, 0, j - (NT - NT_LAT)) * TM
    ys = []
    for g, w in enumerate(POOL_WINDOWS):
        lanes = slice(g * POOL_GROUP, (g + 1) * POOL_GROUP)
        acc = ext_ref[HALO - w // 2:HALO - w // 2 + TM, lanes]
        for s in range(1 - w // 2, w - w // 2):
            acc = acc + ext_ref[HALO + s:HALO + s + TM, lanes]
        cnt = jnp.minimum(pos + (w - w // 2), seq_len) - jnp.maximum(pos - w // 2, 0)
        d = acc / cnt.astype(F32) - u[:, lanes]
        ys.append(_dot(d.astype(BF16), wpool_ref[g]))
    yp = jnp.concatenate(ys, axis=1) * pscale_ref[...]
    o_pool = _dot(yp.astype(BF16), wopool_ref[...])

    merged = gates[:, :D_MODEL] * att + gates[:, D_MODEL:] * o_pool
    y = _dot(merged.astype(BF16), wout_ref[...])
    out_ref[0] = x + m[2:3] * _rms(y, gpost_ref[...])


def _mix_call(xs, mod, gpre, wg, o, o_ctx, womla, u, wpool, pscale, wopool, wout, gpost):
    with_ctx = o_ctx is not None
    n_t = NT if with_ctx else NT_LAT
    off = 0 if with_ctx else NT - NT_LAT
    lat0 = NT - NT_LAT
    hb = TM // HALO
    o_specs = [pl.BlockSpec((1, TM, N_HEADS * V_DIM),
                            lambda b, j: (b, jnp.maximum(j + off - lat0, 0), 0))]
    o_args = [o]
    if with_ctx:
        o_specs.append(pl.BlockSpec((1, CTX_LEN, N_HEADS * V_DIM), lambda b, j: (b, 0, 0)))
        o_args.append(o_ctx)
    return pl.pallas_call(
        functools.partial(_mix_kernel, tile_off=off, with_ctx=with_ctx),
        grid=(BATCH, n_t),
        in_specs=[
            pl.BlockSpec((1, TM, D_MODEL), lambda b, j: (b, j + off, 0)),
            pl.BlockSpec((1, N_MOD, D_MODEL), lambda b, j: (_mod_row(b, j + off), 0, 0)),
            _const_spec((1, D_MODEL)),
            _const_spec(wg.shape),
            *o_specs,
            _const_spec(womla.shape),
            pl.BlockSpec((1, TM, POOL_W), lambda b, j: (b, j + off, 0)),
            pl.BlockSpec((1, HALO, POOL_W),
                         lambda b, j: (b, jnp.maximum((j + off) * hb - 1, 0), 0)),
            pl.BlockSpec((1, HALO, POOL_W),
                         lambda b, j: (b, jnp.minimum((j + off + 1) * hb, TS // HALO - 1), 0)),
            _const_spec(wpool.shape),
            _const_spec((1, POOL_W)),
            _const_spec(wopool.shape),
            _const_spec(wout.shape),
            _const_spec((1, D_MODEL)),
        ],
        out_specs=pl.BlockSpec((1, TM, D_MODEL), lambda b, j: (b, j, 0)),
        out_shape=jax.ShapeDtypeStruct((BATCH, n_t * TM, D_MODEL), F32),
        scratch_shapes=[pltpu.VMEM((TM + 2 * HALO, POOL_W), F32)],
        compiler_params=pltpu.CompilerParams(
            dimension_semantics=("arbitrary", "arbitrary"),
            vmem_limit_bytes=VMEM_LIMIT),
        name="mix",
    )(xs, mod, gpre, wg, *o_args, womla, u, u, u, wpool, pscale, wopool, wout, gpost)


def _ffn_kernel(x_ref, mod_ref, gpre_ref, wgate_ref, wup_ref, wdown_ref, gpost_ref, out_ref):
    x = x_ref[0]
    m = mod_ref[0]
    h = (_rms(x, gpre_ref[...]) * (1.0 + m[3:4]) + m[4:5]).astype(BF16)
    a = _dot(h, wgate_ref[...])
    t = (a * jax.nn.sigmoid(a)) * _dot(h, wup_ref[...])
    y = _dot(t.astype(BF16), wdown_ref[...])
    out_ref[0] = x + m[5:6] * _rms(y, gpost_ref[...])


def _ffn_call(xs, mod, gpre, wgate, wup, wdown, gpost, with_ctx):
    n_t = NT if with_ctx else NT_LAT
    off = 0 if with_ctx else NT - NT_LAT
    tile = lambda b, j: (b, j, 0)
    return pl.pallas_call(
        _ffn_kernel,
        grid=(BATCH, n_t),
        in_specs=[
            pl.BlockSpec((1, TM, D_MODEL), tile),
            pl.BlockSpec((1, N_MOD, D_MODEL), lambda b, j: (_mod_row(b, j + off), 0, 0)),
            _const_spec((1, D_MODEL)),
            _const_spec(wgate.shape),
            _const_spec(wup.shape),
            _const_spec(wdown.shape),
            _const_spec((1, D_MODEL)),
        ],
        out_specs=pl.BlockSpec((1, TM, D_MODEL), tile),
        out_shape=jax.ShapeDtypeStruct((BATCH, n_t * TM, D_MODEL), F32),
        compiler_params=pltpu.CompilerParams(
            dimension_semantics=("arbitrary", "arbitrary"),
            vmem_limit_bytes=VMEM_LIMIT),
        name="ffn",
    )(xs, mod, gpre, wgate, wup, wdown, gpost)


def _rope_tables():
    rows = SEQ // GRID_W
    row = jnp.repeat(jnp.arange(rows), GRID_W)
    col = jnp.tile(jnp.arange(GRID_W), rows)
    pos = jnp.stack([row, col], axis=-1).astype(F32)
    inv_freq = ROPE_THETA ** (-jnp.arange(0, ROPE_AXIS, 2, dtype=F32) / ROPE_AXIS)
    ang = pos[:, :, None] * inv_freq
    cos = jnp.cos(ang)
    sin = jnp.sin(ang)
    cos = jnp.stack([cos, cos], axis=2).reshape(SEQ, QK_ROPE)
    sin = jnp.stack([-sin, sin], axis=2).reshape(SEQ, QK_ROPE)
    cos = jnp.concatenate([jnp.ones((CTX_LEN, QK_ROPE), F32), cos], axis=0)
    sin = jnp.concatenate([jnp.zeros((CTX_LEN, QK_ROPE), F32), sin], axis=0)
    pad = jnp.zeros((TS, QK_ROPE), F32)
    return jnp.concatenate([cos, pad], axis=1), jnp.concatenate([sin, pad], axis=1)


def _rot_cols():
    quarter = QK_ROPE // 4
    idx = []
    for a in range(2):
        base = a * 2 * quarter
        idx += list(range(base + quarter, base + 2 * quarter)) + list(range(base, base + quarter))
    return jnp.asarray(idx, jnp.int32)


def kernel(x, c, ctx, c_ctx, w_ada, b_ada, g_pre_mix, w_in, q_norm, w_uq, kv_norm, w_ukv, w_o_mla,
           w_pool, pool_scale, w_o_pool, w_out, g_post_mix, g_pre_ffn, w_ffn_gate, w_ffn_up,
           w_ffn_down, g_post_ffn):
    assert x.shape == (BATCH, SEQ, D_MODEL) and ctx.shape == (BATCH, CTX_LEN, D_MODEL)
    cvec = jnp.concatenate(
        [c, c_ctx[None, :], jnp.zeros((MOD_ROWS - BATCH - 1, D_MODEL), F32)], axis=0)
    mod_all = _ada_call(cvec, w_ada, b_ada).reshape(DEPTH, MOD_ROWS, N_MOD, D_MODEL)
    cos_t, sin_t = _rope_tables()
    rot = _rot_cols()

    xs = jnp.concatenate([ctx, x], axis=1)
    for l in range(DEPTH):
        last = l == DEPTH - 1
        mod = mod_all[l]
        wl = w_in[l]
        kr_cols = wl[:, OFF_KR:OFF_POOL]
        win_a = jnp.concatenate(
            [wl[:, :OFF_KR], kr_cols, kr_cols[:, rot], wl[:, OFF_POOL:OFF_GATE]], axis=1).astype(BF16)
        wg = wl[:, OFF_GATE:].astype(BF16)
        wq = w_uq[l].reshape(Q_LORA, N_HEADS, QK_NOPE + QK_ROPE)
        wq_rope = wq[:, :, QK_NOPE:]
        wuq_a = jnp.concatenate([wq, wq_rope[:, :, rot]], axis=2).reshape(
            Q_LORA, N_HEADS * HEAD_W).astype(BF16)

        wkv = w_ukv[l].reshape(KV_LORA, N_HEADS, QK_NOPE + V_DIM)
        wuk = wkv[:, :, :QK_NOPE].reshape(KV_LORA, N_HEADS * QK_NOPE).astype(BF16)
        wuvt = wkv[:, :, QK_NOPE:].reshape(KV_LORA, N_HEADS * V_DIM).T.astype(BF16)

        q, k, vt, u = _inproj_call(
            xs, mod, g_pre_mix[l][None], win_a, q_norm[l][None], wuq_a, kv_norm[l][None],
            wuk, wuvt, cos_t, sin_t)
        if last:
            (o,), o_ctx = _attn_call(q, k, vt, ctx_queries=False), None
        else:
            o, o_ctx = _attn_call(q, k, vt, ctx_queries=True)
        xs1 = _mix_call(
            xs, mod, g_pre_mix[l][None], wg, o, o_ctx, w_o_mla[l].astype(BF16), u,
            w_pool[l].astype(BF16), pool_scale[l][None], w_o_pool[l].astype(BF16),
            w_out[l].astype(BF16), g_post_mix[l][None])
        xs = _ffn_call(
            xs1, mod, g_pre_ffn[l][None], w_ffn_gate[l].astype(BF16), w_ffn_up[l].astype(BF16),
            w_ffn_down[l].astype(BF16), g_post_ffn[l][None], with_ctx=not last)
    return xs
```

```python
import functools

import jax
import jax.numpy as jnp
from jax import lax
from jax.experimental import pallas as pl
from jax.experimental.pallas import tpu as pltpu

D_MODEL = 1024
BATCH = 4
SEQ = 4096
DEPTH = 2
GRID_W = 64
CTX_LEN = 256
N_HEADS = 8
QK_NOPE = 128
QK_ROPE = 64
V_DIM = 128
Q_LORA = 256
KV_LORA = 128
POOL_W = 512
POOL_WINDOWS = (2, 4, 8, 16)
N_POOL_GROUPS = 4
POOL_GROUP = POOL_W // N_POOL_GROUPS
D_FF = -(-8 * D_MODEL // (3 * 256)) * 256
N_BRANCH = 2
N_MOD = 6
ROPE_THETA = 10000.0
ROPE_AXIS = QK_ROPE // 2
EPS = 1e-6
ATTN_SCALE = (QK_NOPE + QK_ROPE) ** -0.5
Q_SCALE = ATTN_SCALE * 1.4426950408889634

OFF_KV = Q_LORA
OFF_KR = OFF_KV + KV_LORA
OFF_POOL = OFF_KR + QK_ROPE
OFF_GATE = OFF_POOL + POOL_W

F32 = jnp.float32
BF16 = jnp.bfloat16

TM = 256
TS = CTX_LEN + SEQ
NT = TS // TM
NT_LAT = SEQ // TM
TQ = 256
NQ = SEQ // TQ
HEAD_W = 2 * QK_NOPE
VT_ROWS = V_DIM + 16
HALO = 8
MOD_ROWS = 8
ADA_BN = 1536
VMEM_LIMIT = 52 * 1024 * 1024

assert CTX_LEN == TM and SEQ % TM == 0 and HALO * 2 == max(POOL_WINDOWS)
assert BATCH + 1 <= MOD_ROWS and (N_MOD * D_MODEL) % ADA_BN == 0


def _const_spec(shape):
    nd = len(shape)
    return pl.BlockSpec(shape, lambda *_: (0,) * nd, pipeline_mode=pl.Buffered(1))


def _rms(x, g):
    return x * lax.rsqrt(jnp.mean(x * x, axis=-1, keepdims=True) + EPS) * g


def _dot(a, b):
    return jnp.dot(a, b, preferred_element_type=F32)


def _ada_kernel(c_ref, w_ref, b_ref, o_ref):
    cv = c_ref[...]
    s = cv * jax.nn.sigmoid(cv)
    o_ref[0] = _dot(s.astype(BF16), w_ref[0].astype(BF16)) + b_ref[0]


def _ada_call(cvec, w_ada, b_ada):
    n_out = N_MOD * D_MODEL
    return pl.pallas_call(
        _ada_kernel,
        grid=(DEPTH, n_out // ADA_BN),
        in_specs=[
            pl.BlockSpec((MOD_ROWS, D_MODEL), lambda l, n: (0, 0)),
            pl.BlockSpec((1, D_MODEL, ADA_BN), lambda l, n: (l, 0, n)),
            pl.BlockSpec((1, 1, ADA_BN), lambda l, n: (l, 0, n)),
        ],
        out_specs=pl.BlockSpec((1, MOD_ROWS, ADA_BN), lambda l, n: (l, 0, n)),
        out_shape=jax.ShapeDtypeStruct((DEPTH, MOD_ROWS, n_out), F32),
        compiler_params=pltpu.CompilerParams(
            dimension_semantics=("arbitrary", "arbitrary"),
            vmem_limit_bytes=VMEM_LIMIT),
        name="ada_mod",
    )(cvec, w_ada, b_ada.reshape(DEPTH, 1, n_out))


def _mod_row(b, j):
    return jnp.where(j == 0, BATCH, b)


def _rope(t, cos, sin):
    return t * cos + pltpu.roll(t, QK_ROPE, axis=1) * sin


def _inproj_kernel(x_ref, mod_ref, gpre_ref, win_ref, qn_ref, wuq_ref, kvn_ref,
                   wuk_ref, wuvt_ref, cos_ref, sin_ref, q_ref, k_ref, vt_ref, u_ref):
    x = x_ref[0]
    m = mod_ref[0]
    h = _rms(x, gpre_ref[...]) * (1.0 + m[0:1]) + m[1:2]
    z = _dot(h.astype(BF16), win_ref[...])
    u_ref[0] = z[:, OFF_POOL + QK_ROPE:]
    cos = cos_ref[...]
    sin = sin_ref[...]

    qn = _rms(z[:, :OFF_KV], qn_ref[...])
    qa = _dot(qn.astype(BF16), wuq_ref[...])
    cn = _rms(z[:, OFF_KV:OFF_KR], kvn_ref[...])
    kn = _dot(cn.astype(BF16), wuk_ref[...])
    vt = _dot(wuvt_ref[...], cn.T.astype(BF16))
    kr = _rope(z[:, OFF_KR:OFF_KR + 2 * QK_ROPE], cos, sin).astype(BF16)
    ones_row = (lax.broadcasted_iota(jnp.int32, (VT_ROWS - V_DIM, TM), 0) == 0).astype(BF16)
    for hd in range(N_HEADS):
        c0 = hd * HEAD_W
        q_ref[0, hd, :, :QK_NOPE] = (qa[:, c0:c0 + QK_NOPE] * Q_SCALE).astype(BF16)
        q_ref[0, hd, :, QK_NOPE:] = (
            _rope(qa[:, c0 + QK_NOPE:c0 + HEAD_W], cos, sin) * Q_SCALE).astype(BF16)
        k_ref[0, hd, :, :QK_NOPE] = kn[:, hd * QK_NOPE:(hd + 1) * QK_NOPE].astype(BF16)
        k_ref[0, hd, :, QK_NOPE:] = kr
        vt_ref[0, hd, :V_DIM] = vt[hd * V_DIM:(hd + 1) * V_DIM].astype(BF16)
        vt_ref[0, hd, V_DIM:] = ones_row


def _inproj_call(xs, mod, gpre, win_a, qnorm, wuq_a, kvnorm, wuk, wuvt, cos_t, sin_t):
    tile = lambda b, j: (b, j, 0)
    heads = lambda b, j: (b, 0, j, 0)
    return pl.pallas_call(
        _inproj_kernel,
        grid=(BATCH, NT),
        in_specs=[
            pl.BlockSpec((1, TM, D_MODEL), tile),
            pl.BlockSpec((1, N_MOD, D_MODEL), lambda b, j: (_mod_row(b, j), 0, 0)),
            _const_spec((1, D_MODEL)),
            _const_spec(win_a.shape),
            _const_spec((1, Q_LORA)),
            _const_spec(wuq_a.shape),
            _const_spec((1, KV_LORA)),
            _const_spec(wuk.shape),
            _const_spec(wuvt.shape),
            pl.BlockSpec((TM, 2 * QK_ROPE), lambda b, j: (j, 0)),
            pl.BlockSpec((TM, 2 * QK_ROPE), lambda b, j: (j, 0)),
        ],
        out_specs=[
            pl.BlockSpec((1, N_HEADS, TM, HEAD_W), heads),
            pl.BlockSpec((1, N_HEADS, TM, HEAD_W), heads),
            pl.BlockSpec((1, N_HEADS, VT_ROWS, TM), lambda b, j: (b, 0, 0, j)),
            pl.BlockSpec((1, TM, POOL_W), tile),
        ],
        out_shape=[
            jax.ShapeDtypeStruct((BATCH, N_HEADS, TS, HEAD_W), BF16),
            jax.ShapeDtypeStruct((BATCH, N_HEADS, TS, HEAD_W), BF16),
            jax.ShapeDtypeStruct((BATCH, N_HEADS, VT_ROWS, TS), BF16),
            jax.ShapeDtypeStruct((BATCH, TS, POOL_W), F32),
        ],
        compiler_params=pltpu.CompilerParams(
            dimension_semantics=("arbitrary", "arbitrary"),
            vmem_limit_bytes=VMEM_LIMIT),
        name="inproj",
    )(xs, mod, gpre, win_a, qnorm, wuq_a, kvnorm, wuk, wuvt, cos_t, sin_t)


def _scores_t(k, q):
    return lax.dot_general(k, q, (((1,), (1,)), ((), ())), preferred_element_type=F32)


MAX_ACC_ROWS = 32


def _col_max(s):
    m = s[:MAX_ACC_ROWS]
    for r in range(MAX_ACC_ROWS, s.shape[0], MAX_ACC_ROWS):
        m = jnp.maximum(m, s[r:r + MAX_ACC_ROWS])
    return jnp.max(m, axis=0, keepdims=True)


def _softmax_pv_t(s, m, vt):
    p = jnp.exp2(s - m)
    ot = _dot(vt, p.astype(BF16))
    return (ot[:V_DIM] / ot[V_DIM:V_DIM + 1]).T


def _attn_kernel(q_ref, k_ref, vt_ref, o_ref, *rest, ctx_queries):
    if ctx_queries:
        octx_ref, sa_ref, sb_ref, ma_ref, mb_ref = rest
    else:
        sa_ref, sb_ref, ma_ref, mb_ref = rest
    b, h, i = pl.program_id(0), pl.program_id(1), pl.program_id(2)

    def scores_stage(t, s_ref, m_ref):
        r = pl.multiple_of(CTX_LEN + t * TQ, TM)
        s = _scores_t(k_ref[0, 0], q_ref[0, 0, pl.ds(r, TQ), :])
        s_ref[...] = s
        m_ref[...] = _col_max(s)

    def output_stage(s_ref, m_ref):
        o_ref[0] = _softmax_pv_t(s_ref[...], m_ref[...], vt_ref[0, 0]).astype(BF16)

    @pl.when(jnp.logical_and(jnp.logical_and(b == 0, h == 0), i == 0))
    def _():
        scores_stage(0, sa_ref, ma_ref)

    if ctx_queries:
        @pl.when(i == 0)
        def _():
            sc = _scores_t(k_ref[0, 0, :CTX_LEN], q_ref[0, 0, :CTX_LEN])
            octx_ref[0] = _softmax_pv_t(
                sc, _col_max(sc), vt_ref[0, 0, :, :CTX_LEN]).astype(BF16)

    nxt = jnp.where(i == NQ - 1, 0, i + 1)

    @pl.when(i % 2 == 0)
    def _():
        scores_stage(nxt, sb_ref, mb_ref)
        output_stage(sa_ref, ma_ref)

    @pl.when(i % 2 == 1)
    def _():
        scores_stage(nxt, sa_ref, ma_ref)
        output_stage(sb_ref, mb_ref)


def _next_head_index(b, h, i):
    flat = b * N_HEADS + h
    flat = jnp.where(i == NQ - 1, jnp.minimum(flat + 1, BATCH * N_HEADS - 1), flat)
    return (flat // N_HEADS, flat % N_HEADS, 0, 0)


def _attn_call(q, k, vt, ctx_queries):
    assert NQ % 2 == 0
    out_specs = [pl.BlockSpec((1, TQ, V_DIM), lambda b, h, i: (b, i, h))]
    out_shape = [jax.ShapeDtypeStruct((BATCH, SEQ, N_HEADS * V_DIM), BF16)]
    if ctx_queries:
        out_specs.append(pl.BlockSpec((1, CTX_LEN, V_DIM), lambda b, h, i: (b, 0, h)))
        out_shape.append(jax.ShapeDtypeStruct((BATCH, CTX_LEN, N_HEADS * V_DIM), BF16))
    return pl.pallas_call(
        functools.partial(_attn_kernel, ctx_queries=ctx_queries),
        grid=(BATCH, N_HEADS, NQ),
        in_specs=[
            pl.BlockSpec((1, 1, TS, HEAD_W), _next_head_index),
            pl.BlockSpec((1, 1, TS, HEAD_W), _next_head_index),
            pl.BlockSpec((1, 1, VT_ROWS, TS), lambda b, h, i: (b, h, 0, 0)),
        ],
        out_specs=out_specs,
        out_shape=out_shape,
        scratch_shapes=[pltpu.VMEM((TS, TQ), F32), pltpu.VMEM((TS, TQ), F32),
                        pltpu.VMEM((1, TQ), F32), pltpu.VMEM((1, TQ), F32)],
        compiler_params=pltpu.CompilerParams(
            dimension_semantics=("arbitrary", "arbitrary", "arbitrary"),
            vmem_limit_bytes=VMEM_LIMIT),
        name="attention",
    )(q, k, vt)


def _mix_kernel(x_ref, mod_ref, gpre_ref, wg_ref, o_ref, *rest, tile_off, with_ctx):
    if with_ctx:
        octx_ref, *rest = rest
    (womla_ref, uc_ref, up_ref, un_ref, wpool_ref, pscale_ref, wopool_ref, wout_ref,
     gpost_ref, out_ref, ext_ref) = rest
    j = pl.program_id(1) + tile_off
    x = x_ref[0]
    m = mod_ref[0]
    h = (_rms(x, gpre_ref[...]) * (1.0 + m[0:1]) + m[1:2]).astype(BF16)
    gates = jax.nn.sigmoid(_dot(h, wg_ref[...]))
    o = jnp.where(j == 0, octx_ref[0], o_ref[0]) if with_ctx else o_ref[0]
    att = _dot(o, womla_ref[...])

    is_ctx = j == 0
    first = jnp.logical_or(is_ctx, j == NT - NT_LAT)
    last = jnp.logical_or(is_ctx, j == NT - 1)
    u = uc_ref[0]
    ext_ref[0:HALO] = jnp.where(first, 0.0, up_ref[0])
    ext_ref[HALO:HALO + TM] = u
    ext_ref[HALO + TM:] = jnp.where(last, 0.0, un_ref[0])
    seq_len = jnp.where(is_ctx, CTX_LEN, SEQ)
    pos = lax.broadcasted_iota(jnp.int32, (TM, 1), 0) + jnp.where(is_ctx, 0, j - (NT - NT_LAT)) * TM
    ys = []
    for g, w in enumerate(POOL_WINDOWS):
        lanes = slice(g * POOL_GROUP, (g + 1) * POOL_GROUP)
        acc = ext_ref[HALO - w // 2:HALO - w // 2 + TM, lanes]
        for s in range(1 - w // 2, w - w // 2):
            acc = acc + ext_ref[HALO + s:HALO + s + TM, lanes]
        cnt = jnp.minimum(pos + (w - w // 2), seq_len) - jnp.maximum(pos - w // 2, 0)
        d = acc / cnt.astype(F32) - u[:, lanes]
        ys.append(_dot(d.astype(BF16), wpool_ref[g]))
    yp = jnp.concatenate(ys, axis=1) * pscale_ref[...]
    o_pool = _dot(yp.astype(BF16), wopool_ref[...])

    merged = gates[:, :D_MODEL] * att + gates[:, D_MODEL:] * o_pool
    y = _dot(merged.astype(BF16), wout_ref[...])
    out_ref[0] = x + m[2:3] * _rms(y, gpost_ref[...])


def _mix_call(xs, mod, gpre, wg, o, o_ctx, womla, u, wpool, pscale, wopool, wout, gpost):
    with_ctx = o_ctx is not None
    n_t = NT if with_ctx else NT_LAT
    off = 0 if with_ctx else NT - NT_LAT
    lat0 = NT - NT_LAT
    hb = TM // HALO
    o_specs = [pl.BlockSpec((1, TM, N_HEADS * V_DIM),
                            lambda b, j: (b, jnp.maximum(j + off - lat0, 0), 0))]
    o_args = [o]
    if with_ctx:
        o_specs.append(pl.BlockSpec((1, CTX_LEN, N_HEADS * V_DIM), lambda b, j: (b, 0, 0)))
        o_args.append(o_ctx)
    return pl.pallas_call(
        functools.partial(_mix_kernel, tile_off=off, with_ctx=with_ctx),
        grid=(BATCH, n_t),
        in_specs=[
            pl.BlockSpec((1, TM, D_MODEL), lambda b, j: (b, j + off, 0)),
            pl.BlockSpec((1, N_MOD, D_MODEL), lambda b, j: (_mod_row(b, j + off), 0, 0)),
            _const_spec((1, D_MODEL)),
            _const_spec(wg.shape),
            *o_specs,
            _const_spec(womla.shape),
            pl.BlockSpec((1, TM, POOL_W), lambda b, j: (b, j + off, 0)),
            pl.BlockSpec((1, HALO, POOL_W),
                         lambda b, j: (b, jnp.maximum((j + off) * hb - 1, 0), 0)),
            pl.BlockSpec((1, HALO, POOL_W),
                         lambda b, j: (b, jnp.minimum((j + off + 1) * hb, TS // HALO - 1), 0)),
            _const_spec(wpool.shape),
            _const_spec((1, POOL_W)),
            _const_spec(wopool.shape),
            _const_spec(wout.shape),
            _const_spec((1, D_MODEL)),
        ],
        out_specs=pl.BlockSpec((1, TM, D_MODEL), lambda b, j: (b, j, 0)),
        out_shape=jax.ShapeDtypeStruct((BATCH, n_t * TM, D_MODEL), F32),
        scratch_shapes=[pltpu.VMEM((TM + 2 * HALO, POOL_W), F32)],
        compiler_params=pltpu.CompilerParams(
            dimension_semantics=("arbitrary", "arbitrary"),
            vmem_limit_bytes=VMEM_LIMIT),
        name="mix",
    )(xs, mod, gpre, wg, *o_args, womla, u, u, u, wpool, pscale, wopool, wout, gpost)


def _ffn_kernel(x_ref, mod_ref, gpre_ref, wgate_ref, wup_ref, wdown_ref, gpost_ref, out_ref):
    x = x_ref[0]
    m = mod_ref[0]
    h = (_rms(x, gpre_ref[...]) * (1.0 + m[3:4]) + m[4:5]).astype(BF16)
    a = _dot(h, wgate_ref[...])
    t = (a * jax.nn.sigmoid(a)) * _dot(h, wup_ref[...])
    y = _dot(t.astype(BF16), wdown_ref[...])
    out_ref[0] = x + m[5:6] * _rms(y, gpost_ref[...])


def _ffn_call(xs, mod, gpre, wgate, wup, wdown, gpost, with_ctx):
    n_t = NT if with_ctx else NT_LAT
    off = 0 if with_ctx else NT - NT_LAT
    tile = lambda b, j: (b, j, 0)
    return pl.pallas_call(
        _ffn_kernel,
        grid=(BATCH, n_t),
        in_specs=[
            pl.BlockSpec((1, TM, D_MODEL), tile),
            pl.BlockSpec((1, N_MOD, D_MODEL), lambda b, j: (_mod_row(b, j + off), 0, 0)),
            _const_spec((1, D_MODEL)),
            _const_spec(wgate.shape),
            _const_spec(wup.shape),
            _const_spec(wdown.shape),
            _const_spec((1, D_MODEL)),
        ],
        out_specs=pl.BlockSpec((1, TM, D_MODEL), tile),
        out_shape=jax.ShapeDtypeStruct((BATCH, n_t * TM, D_MODEL), F32),
        compiler_params=pltpu.CompilerParams(
            dimension_semantics=("arbitrary", "arbitrary"),
            vmem_limit_bytes=VMEM_LIMIT),
        name="ffn",
    )(xs, mod, gpre, wgate, wup, wdown, gpost)


def _rope_tables():
    rows = SEQ // GRID_W
    row = jnp.repeat(jnp.arange(rows), GRID_W)
    col = jnp.tile(jnp.arange(GRID_W), rows)
    pos = jnp.stack([row, col], axis=-1).astype(F32)
    inv_freq = ROPE_THETA ** (-jnp.arange(0, ROPE_AXIS, 2, dtype=F32) / ROPE_AXIS)
    ang = pos[:, :, None] * inv_freq
    cos = jnp.cos(ang)
    sin = jnp.sin(ang)
    cos = jnp.stack([cos, cos], axis=2).reshape(SEQ, QK_ROPE)
    sin = jnp.stack([-sin, sin], axis=2).reshape(SEQ, QK_ROPE)
    cos = jnp.concatenate([jnp.ones((CTX_LEN, QK_ROPE), F32), cos], axis=0)
    sin = jnp.concatenate([jnp.zeros((CTX_LEN, QK_ROPE), F32), sin], axis=0)
    pad = jnp.zeros((TS, QK_ROPE), F32)
    return jnp.concatenate([cos, pad], axis=1), jnp.concatenate([sin, pad], axis=1)


def _rot_cols():
    quarter = QK_ROPE // 4
    idx = []
    for a in range(2):
        base = a * 2 * quarter
        idx += list(range(base + quarter, base + 2 * quarter)) + list(range(base, base + quarter))
    return jnp.asarray(idx, jnp.int32)


def kernel(x, c, ctx, c_ctx, w_ada, b_ada, g_pre_mix, w_in, q_norm, w_uq, kv_norm, w_ukv, w_o_mla,
           w_pool, pool_scale, w_o_pool, w_out, g_post_mix, g_pre_ffn, w_ffn_gate, w_ffn_up,
           w_ffn_down, g_post_ffn):
    assert x.shape == (BATCH, SEQ, D_MODEL) and ctx.shape == (BATCH, CTX_LEN, D_MODEL)
    cvec = jnp.concatenate(
        [c, c_ctx[None, :], jnp.zeros((MOD_ROWS - BATCH - 1, D_MODEL), F32)], axis=0)
    mod_all = _ada_call(cvec, w_ada, b_ada).reshape(DEPTH, MOD_ROWS, N_MOD, D_MODEL)
    cos_t, sin_t = _rope_tables()
    rot = _rot_cols()

    xs = jnp.concatenate([ctx, x], axis=1)
    for l in range(DEPTH):
        last = l == DEPTH - 1
        mod = mod_all[l]
        wl = w_in[l]
        kr_cols = wl[:, OFF_KR:OFF_POOL]
        win_a = jnp.concatenate(
            [wl[:, :OFF_KR], kr_cols, kr_cols[:, rot], wl[:, OFF_POOL:OFF_GATE]], axis=1).astype(BF16)
        wg = wl[:, OFF_GATE:].astype(BF16)
        wq = w_uq[l].reshape(Q_LORA, N_HEADS, QK_NOPE + QK_ROPE)
        wq_rope = wq[:, :, QK_NOPE:]
        wuq_a = jnp.concatenate([wq, wq_rope[:, :, rot]], axis=2).reshape(
            Q_LORA, N_HEADS * HEAD_W).astype(BF16)

        wkv = w_ukv[l].reshape(KV_LORA, N_HEADS, QK_NOPE + V_DIM)
        wuk = wkv[:, :, :QK_NOPE].reshape(KV_LORA, N_HEADS * QK_NOPE).astype(BF16)
        wuvt = wkv[:, :, QK_NOPE:].reshape(KV_LORA, N_HEADS * V_DIM).T.astype(BF16)

        q, k, vt, u = _inproj_call(
            xs, mod, g_pre_mix[l][None], win_a, q_norm[l][None], wuq_a, kv_norm[l][None],
            wuk, wuvt, cos_t, sin_t)
        if last:
            (o,), o_ctx = _attn_call(q, k, vt, ctx_queries=False), None
        else:
            o, o_ctx = _attn_call(q, k, vt, ctx_queries=True)
        xs1 = _mix_call(
            xs, mod, g_pre_mix[l][None], wg, o, o_ctx, w_o_mla[l].astype(BF16), u,
            w_pool[l].astype(BF16), pool_scale[l][None], w_o_pool[l].astype(BF16),
            w_out[l].astype(BF16), g_post_mix[l][None])
        xs = _ffn_call(
            xs1, mod, g_pre_ffn[l][None], w_ffn_gate[l].astype(BF16), w_ffn_up[l].astype(BF16),
            w_ffn_down[l].astype(BF16), g_post_ffn[l][None], with_ctx=not last)
    return xs
```

```python
import functools

import jax
import jax.numpy as jnp
import numpy as np
from jax import lax
from jax.experimental import pallas as pl
from jax.experimental.pallas import tpu as pltpu

D_MODEL = 1024
BATCH = 4
SEQ = 4096
DEPTH = 2
GRID_W = 64
CTX_LEN = 256
N_HEADS = 8
QK_NOPE = 128
QK_ROPE = 64
V_DIM = 128
Q_LORA = 256
KV_LORA = 128
POOL_W = 512
POOL_WINDOWS = (2, 4, 8, 16)
N_POOL_GROUPS = 4
POOL_GROUP = POOL_W // N_POOL_GROUPS
D_FF = -(-8 * D_MODEL // (3 * 256)) * 256
N_BRANCH = 2
N_MOD = 6
ROPE_THETA = 10000.0
ROPE_AXIS = QK_ROPE // 2
EPS = 1e-6
ATTN_SCALE = (QK_NOPE + QK_ROPE) ** -0.5
Q_SCALE = ATTN_SCALE * 1.4426950408889634

OFF_KV = Q_LORA
OFF_KR = OFF_KV + KV_LORA
OFF_POOL = OFF_KR + QK_ROPE
OFF_GATE = OFF_POOL + POOL_W

F32 = jnp.float32
BF16 = jnp.bfloat16

TM = 256
TS = CTX_LEN + SEQ
NT = TS // TM
NT_LAT = SEQ // TM
LAT0 = NT - NT_LAT
TQ = TM
NQ = SEQ // TQ
HEAD_W = 2 * KV_LORA
KV_W = QK_NOPE + V_DIM
CT_ROWS = KV_LORA + 16
O_W = N_HEADS * KV_LORA
HALO = 8
MOD_ROWS = 8
ADA_BN = 1536
MAX_ACC_ROWS = 32
VMEM_LIMIT = 52 * 1024 * 1024

assert CTX_LEN == TM and SEQ % TM == 0 and HALO * 2 == max(POOL_WINDOWS)
assert BATCH + 1 <= MOD_ROWS and (N_MOD * D_MODEL) % ADA_BN == 0
assert QK_NOPE == KV_LORA == V_DIM and 2 * QK_ROPE == KV_LORA
assert NQ % 2 == 0


def _layer_spec(arr, l):
    nd = arr.ndim
    return pl.BlockSpec((1,) + arr.shape[1:], lambda *_: (l,) + (0,) * (nd - 1),
                        pipeline_mode=pl.Buffered(1))


def _params(n_grid):
    return pltpu.CompilerParams(dimension_semantics=("arbitrary",) * n_grid,
                                vmem_limit_bytes=VMEM_LIMIT)


def _rms(x, g):
    return x * lax.rsqrt(jnp.mean(x * x, axis=-1, keepdims=True) + EPS) * g


def _dot(a, b):
    return jnp.dot(a, b, preferred_element_type=F32)


def _dot_t(a, b):
    return lax.dot_general(a, b, (((1,), (1,)), ((), ())), preferred_element_type=F32)


def _ada_kernel(c_ref, w_ref, b_ref, o_ref):
    cv = c_ref[...]
    s = cv * jax.nn.sigmoid(cv)
    o_ref[0] = _dot(s.astype(BF16), w_ref[0].astype(BF16)) + b_ref[0]


def _ada_call(cvec, w_ada, b_ada):
    n_out = N_MOD * D_MODEL
    return pl.pallas_call(
        _ada_kernel,
        grid=(DEPTH, n_out // ADA_BN),
        in_specs=[
            pl.BlockSpec((MOD_ROWS, D_MODEL), lambda l, n: (0, 0)),
            pl.BlockSpec((1, D_MODEL, ADA_BN), lambda l, n: (l, 0, n)),
            pl.BlockSpec((1, 1, ADA_BN), lambda l, n: (l, 0, n)),
        ],
        out_specs=pl.BlockSpec((1, MOD_ROWS, ADA_BN), lambda l, n: (l, 0, n)),
        out_shape=jax.ShapeDtypeStruct((DEPTH, MOD_ROWS, n_out), F32),
        compiler_params=_params(2),
        name="ada_mod",
    )(cvec, w_ada, b_ada.reshape(DEPTH, 1, n_out))


def _mod_spec(l, off):
    return pl.BlockSpec((1, 1, N_MOD, D_MODEL),
                        lambda b, j: (l, jnp.where(j + off == 0, BATCH, b), 0, 0))


def _fold_kernel(wqn_ref, wqr_ref, wukv_ref, womla_ref, wqf_ref, weff_ref):
    for hd in range(N_HEADS):
        n0 = hd * QK_NOPE
        c0 = hd * KV_W
        q0 = hd * HEAD_W
        wuk = wukv_ref[0, :, c0:c0 + QK_NOPE].astype(BF16)
        wuv = wukv_ref[0, :, c0 + QK_NOPE:c0 + KV_W].astype(BF16)
        wqf_ref[0, :, q0:q0 + KV_LORA] = _dot_t(
            wqn_ref[0, :, n0:n0 + QK_NOPE].astype(BF16), wuk).astype(BF16)
        wqf_ref[0, :, q0 + KV_LORA:q0 + HEAD_W] = wqr_ref[0, :, n0:n0 + 2 * QK_ROPE].astype(BF16)
        weff_ref[0, hd * KV_LORA:(hd + 1) * KV_LORA, :] = _dot(
            wuv, womla_ref[0, hd * V_DIM:(hd + 1) * V_DIM, :].astype(BF16)).astype(BF16)


def _fold_call(wqn, wqr, w_ukv, w_o_mla):
    layer = lambda a: pl.BlockSpec((1,) + a.shape[1:], lambda l: (l, 0, 0))
    return pl.pallas_call(
        _fold_kernel,
        grid=(DEPTH,),
        in_specs=[layer(wqn), layer(wqr), layer(w_ukv), layer(w_o_mla)],
        out_specs=[pl.BlockSpec((1, Q_LORA, N_HEADS * HEAD_W), lambda l: (l, 0, 0)),
                   pl.BlockSpec((1, O_W, D_MODEL), lambda l: (l, 0, 0))],
        out_shape=[jax.ShapeDtypeStruct((DEPTH, Q_LORA, N_HEADS * HEAD_W), BF16),
                   jax.ShapeDtypeStruct((DEPTH, O_W, D_MODEL), BF16)],
        compiler_params=_params(1),
        name="fold",
    )(wqn, wqr, w_ukv, w_o_mla)


def _row_specs(separate, off):
    if separate:
        return [pl.BlockSpec((1, TM, D_MODEL), lambda b, j: (b, 0, 0)),
                pl.BlockSpec((1, TM, D_MODEL), lambda b, j: (b, jnp.maximum(j + off - LAT0, 0), 0))]
    return [pl.BlockSpec((1, TM, D_MODEL), lambda b, j: (b, j + off, 0))]


def _load_rows(x_refs, j):
    if len(x_refs) == 2:
        return jnp.where(j == 0, x_refs[0][0], x_refs[1][0])
    return x_refs[0][0]


def _rope(t, cos, sin):
    return t * cos + pltpu.roll(t, QK_ROPE, axis=1) * sin


def _inproj_kernel(*refs, n_x):
    x_refs = refs[:n_x]
    (mod_ref, gpre_ref, win_ref, qn_ref, wqf_ref, kvn_ref, cos_ref, sin_ref,
     q_ref, k_ref, ct_ref, u_ref) = refs[n_x:]
    x = _load_rows(x_refs, pl.program_id(1))
    m = mod_ref[0, 0]
    h = _rms(x, gpre_ref[0]) * (1.0 + m[0:1]) + m[1:2]
    z = _dot(h.astype(BF16), win_ref[0])
    u_ref[0] = z[:, OFF_POOL + QK_ROPE:]
    cos = cos_ref[...]
    sin = sin_ref[...]

    qn = _rms(z[:, :OFF_KV], qn_ref[0])
    qa = _dot(qn.astype(BF16), wqf_ref[0])
    for hd in range(N_HEADS):
        c0 = hd * HEAD_W
        q_ref[0, hd, :, :KV_LORA] = (qa[:, c0:c0 + KV_LORA] * Q_SCALE).astype(BF16)
        q_ref[0, hd, :, KV_LORA:] = (
            _rope(qa[:, c0 + KV_LORA:c0 + HEAD_W], cos, sin) * Q_SCALE).astype(BF16)

    cn = _rms(z[:, OFF_KV:OFF_KR], kvn_ref[0])
    k_ref[0, :, :KV_LORA] = cn.astype(BF16)
    k_ref[0, :, KV_LORA:] = _rope(z[:, OFF_KR:OFF_KR + 2 * QK_ROPE], cos, sin).astype(BF16)
    ct_ref[0, :KV_LORA] = cn.T.astype(BF16)
    ct_ref[0, KV_LORA:] = (
        lax.broadcasted_iota(jnp.int32, (CT_ROWS - KV_LORA, TM), 0) == 0).astype(BF16)


def _inproj_call(l, x_args, mod_all, gpre, win_a, qnorm, wqf, kvnorm, cos_t, sin_t):
    tile = lambda b, j: (b, j, 0)
    return pl.pallas_call(
        functools.partial(_inproj_kernel, n_x=len(x_args)),
        grid=(BATCH, NT),
        in_specs=[
            *_row_specs(len(x_args) == 2, 0),
            _mod_spec(l, 0),
            _layer_spec(gpre, l),
            _layer_spec(win_a, l),
            _layer_spec(qnorm, l),
            _layer_spec(wqf, l),
            _layer_spec(kvnorm, l),
            pl.BlockSpec((TM, 2 * QK_ROPE), lambda b, j: (j, 0)),
            pl.BlockSpec((TM, 2 * QK_ROPE), lambda b, j: (j, 0)),
        ],
        out_specs=[
            pl.BlockSpec((1, N_HEADS, TM, HEAD_W), lambda b, j: (b, 0, j, 0)),
            pl.BlockSpec((1, TM, HEAD_W), tile),
            pl.BlockSpec((1, CT_ROWS, TM), lambda b, j: (b, 0, j)),
            pl.BlockSpec((1, TM, POOL_W), tile),
        ],
        out_shape=[
            jax.ShapeDtypeStruct((BATCH, N_HEADS, TS, HEAD_W), BF16),
            jax.ShapeDtypeStruct((BATCH, TS, HEAD_W), BF16),
            jax.ShapeDtypeStruct((BATCH, CT_ROWS, TS), BF16),
            jax.ShapeDtypeStruct((BATCH, TS, POOL_W), F32),
        ],
        compiler_params=_params(2),
        name="inproj",
    )(*x_args, mod_all, gpre, win_a, qnorm, wqf, kvnorm, cos_t, sin_t)


def _col_max(s):
    m = s[:MAX_ACC_ROWS]
    for r in range(MAX_ACC_ROWS, s.shape[0], MAX_ACC_ROWS):
        m = jnp.maximum(m, s[r:r + MAX_ACC_ROWS])
    return jnp.max(m, axis=0, keepdims=True)


def _softmax_pc_t(s, m, ct):
    p = jnp.exp2(s - m)
    ot = _dot(ct, p.astype(BF16))
    return (ot[:KV_LORA] / ot[KV_LORA:KV_LORA + 1]).T


def _attn_kernel(qn_ref, q0_ref, *rest, ctx_queries):
    if ctx_queries:
        qc_ref, k_ref, ct_ref, o_ref, octx_ref, sa_ref, sb_ref, ma_ref, mb_ref = rest
    else:
        k_ref, ct_ref, o_ref, sa_ref, sb_ref, ma_ref, mb_ref = rest
    b, h, i = pl.program_id(0), pl.program_id(1), pl.program_id(2)

    def scores_stage(q_ref, s_ref, m_ref):
        s = _dot_t(k_ref[0], q_ref[0, 0])
        s_ref[...] = s
        m_ref[...] = _col_max(s)

    def output_stage(s_ref, m_ref):
        o_ref[0] = _softmax_pc_t(s_ref[...], m_ref[...], ct_ref[0]).astype(BF16)

    @pl.when(jnp.logical_and(jnp.logical_and(b == 0, h == 0), i == 0))
    def _():
        scores_stage(q0_ref, sa_ref, ma_ref)

    if ctx_queries:
        @pl.when(i == 0)
        def _():
            sc = _dot_t(k_ref[0, :CTX_LEN], qc_ref[0, 0])
            octx_ref[0] = _softmax_pc_t(sc, _col_max(sc), ct_ref[0, :, :CTX_LEN]).astype(BF16)

    @pl.when(i % 2 == 0)
    def _():
        scores_stage(qn_ref, sb_ref, mb_ref)
        output_stage(sa_ref, ma_ref)

    @pl.when(i % 2 == 1)
    def _():
        scores_stage(qn_ref, sa_ref, ma_ref)
        output_stage(sb_ref, mb_ref)


def _next_step(b, h, i):
    f = jnp.minimum((b * N_HEADS + h) * NQ + i + 1, BATCH * N_HEADS * NQ - 1)
    return f // (N_HEADS * NQ), (f // NQ) % N_HEADS, f % NQ


def _attn_call(q, k, ct, ctx_queries):
    def next_q(b, h, i):
        nb, nh, nt = _next_step(b, h, i)
        return (nb, nh, nt + LAT0, 0)

    q_tile = (1, 1, TQ, HEAD_W)
    in_specs = [pl.BlockSpec(q_tile, next_q),
                pl.BlockSpec(q_tile, lambda b, h, i: (0, 0, LAT0, 0))]
    args = [q, q]
    out_specs = [pl.BlockSpec((1, TQ, KV_LORA), lambda b, h, i: (b, i, h))]
    out_shape = [jax.ShapeDtypeStruct((BATCH, SEQ, O_W), BF16)]
    if ctx_queries:
        in_specs.append(pl.BlockSpec(q_tile, lambda b, h, i: (b, h, 0, 0)))
        args.append(q)
        out_specs.append(pl.BlockSpec((1, CTX_LEN, KV_LORA), lambda b, h, i: (b, 0, h)))
        out_shape.append(jax.ShapeDtypeStruct((BATCH, CTX_LEN, O_W), BF16))
    in_specs += [
        pl.BlockSpec((1, TS, HEAD_W), lambda b, h, i: (_next_step(b, h, i)[0], 0, 0)),
        pl.BlockSpec((1, CT_ROWS, TS), lambda b, h, i: (b, 0, 0)),
    ]
    return pl.pallas_call(
        functools.partial(_attn_kernel, ctx_queries=ctx_queries),
        grid=(BATCH, N_HEADS, NQ),
        in_specs=in_specs,
        out_specs=out_specs,
        out_shape=out_shape,
        scratch_shapes=[pltpu.VMEM((TS, TQ), F32), pltpu.VMEM((TS, TQ), F32),
                        pltpu.VMEM((1, TQ), F32), pltpu.VMEM((1, TQ), F32)],
        compiler_params=_params(3),
        name="attention",
    )(*args, k, ct)


def _mix_kernel(*refs, n_x, tile_off, with_ctx):
    x_refs = refs[:n_x]
    mod_ref, gpre_ref, wg_ref, o_ref, *rest = refs[n_x:]
    if with_ctx:
        octx_ref, *rest = rest
    (weff_ref, uc_ref, up_ref, un_ref, wpool_ref, pscale_ref, wopool_ref, wout_ref,
     gpost_ref, out_ref, ext_ref) = rest
    j = pl.program_id(1) + tile_off
    x = _load_rows(x_refs, j)
    m = mod_ref[0, 0]
    h = (_rms(x, gpre_ref[0]) * (1.0 + m[0:1]) + m[1:2]).astype(BF16)
    gates = jax.nn.sigmoid(_dot(h, wg_ref[0]))
    o = jnp.where(j == 0, octx_ref[0], o_ref[0]) if with_ctx else o_ref[0]
    att = _dot(o, weff_ref[0])

    is_ctx = j == 0
    first = jnp.logical_or(is_ctx, j == LAT0)
    last = jnp.logical_or(is_ctx, j == NT - 1)
    u = uc_ref[0]
    ext_ref[0:HALO] = jnp.where(first, 0.0, up_ref[0])
    ext_ref[HALO:HALO + TM] = u
    ext_ref[HALO + TM:] = jnp.where(last, 0.0, un_ref[0])
    seq_len = jnp.where(is_ctx, CTX_LEN, SEQ)
    pos = lax.broadcasted_iota(jnp.int32, (TM, 1), 0) + jnp.where(is_ctx, 0, j - LAT0) * TM
    ys = []
    for g, w in enumerate(POOL_WINDOWS):
        lanes = slice(g * POOL_GROUP, (g + 1) * POOL_GROUP)
        acc = ext_ref[HALO - w // 2:HALO - w // 2 + TM, lanes]
        for s in range(1 - w // 2, w - w // 2):
            acc = acc + ext_ref[HALO + s:HALO + s + TM, lanes]
        cnt = jnp.minimum(pos + (w - w // 2), seq_len) - jnp.maximum(pos - w // 2, 0)
        d = acc / cnt.astype(F32) - u[:, lanes]
        ys.append(_dot(d.astype(BF16), wpool_ref[0, g]))
    yp = jnp.concatenate(ys, axis=1) * pscale_ref[0]
    o_pool = _dot(yp.astype(BF16), wopool_ref[0])

    merged = gates[:, :D_MODEL] * att + gates[:, D_MODEL:] * o_pool
    y = _dot(merged.astype(BF16), wout_ref[0])
    out_ref[0] = x + m[2:3] * _rms(y, gpost_ref[0])


def _mix_call(l, x_args, mod_all, gpre, wg, o, o_ctx, weff, u, wpool, pscale, wopool, wout, gpost):
    with_ctx = o_ctx is not None
    n_t = NT if with_ctx else NT_LAT
    off = 0 if with_ctx else LAT0
    hb = TM // HALO
    o_specs = [pl.BlockSpec((1, TM, O_W), lambda b, j: (b, jnp.maximum(j + off - LAT0, 0), 0))]
    o_args = [o]
    if with_ctx:
        o_specs.append(pl.BlockSpec((1, CTX_LEN, O_W), lambda b, j: (b, 0, 0)))
        o_args.append(o_ctx)
    return pl.pallas_call(
        functools.partial(_mix_kernel, n_x=len(x_args), tile_off=off, with_ctx=with_ctx),
        grid=(BATCH, n_t),
        in_specs=[
            *_row_specs(len(x_args) == 2, off),
            _mod_spec(l, off),
            _layer_spec(gpre, l),
            _layer_spec(wg, l),
            *o_specs,
            _layer_spec(weff, l),
            pl.BlockSpec((1, TM, POOL_W), lambda b, j: (b, j + off, 0)),
            pl.BlockSpec((1, HALO, POOL_W),
                         lambda b, j: (b, jnp.maximum((j + off) * hb - 1, 0), 0)),
            pl.BlockSpec((1, HALO, POOL_W),
                         lambda b, j: (b, jnp.minimum((j + off + 1) * hb, TS // HALO - 1), 0)),
            _layer_spec(wpool, l),
            _layer_spec(pscale, l),
            _layer_spec(wopool, l),
            _layer_spec(wout, l),
            _layer_spec(gpost, l),
        ],
        out_specs=pl.BlockSpec((1, TM, D_MODEL), lambda b, j: (b, j, 0)),
        out_shape=jax.ShapeDtypeStruct((BATCH, n_t * TM, D_MODEL), F32),
        scratch_shapes=[pltpu.VMEM((TM + 2 * HALO, POOL_W), F32)],
        compiler_params=_params(2),
        name="mix",
    )(*x_args, mod_all, gpre, wg, *o_args, weff, u, u, u, wpool, pscale, wopool, wout, gpost)


def _ffn_kernel(x_ref, mod_ref, gpre_ref, wgate_ref, wup_ref, wdown_ref, gpost_ref, out_ref):
    x = x_ref[0]
    m = mod_ref[0, 0]
    h = (_rms(x, gpre_ref[0]) * (1.0 + m[3:4]) + m[4:5]).astype(BF16)
    a = _dot(h, wgate_ref[0])
    t = (a * jax.nn.sigmoid(a)) * _dot(h, wup_ref[0])
    y = _dot(t.astype(BF16), wdown_ref[0])
    out_ref[0] = x + m[5:6] * _rms(y, gpost_ref[0])


def _ffn_call(l, xs, mod_all, gpre, wgate, wup, wdown, gpost, with_ctx):
    n_t = NT if with_ctx else NT_LAT
    off = 0 if with_ctx else LAT0
    tile = lambda b, j: (b, j, 0)
    return pl.pallas_call(
        _ffn_kernel,
        grid=(BATCH, n_t),
        in_specs=[
            pl.BlockSpec((1, TM, D_MODEL), tile),
            _mod_spec(l, off),
            _layer_spec(gpre, l),
            _layer_spec(wgate, l),
            _layer_spec(wup, l),
            _layer_spec(wdown, l),
            _layer_spec(gpost, l),
        ],
        out_specs=pl.BlockSpec((1, TM, D_MODEL), tile),
        out_shape=jax.ShapeDtypeStruct((BATCH, n_t * TM, D_MODEL), F32),
        compiler_params=_params(2),
        name="ffn",
    )(xs, mod_all, gpre, wgate, wup, wdown, gpost)


def _rope_tables():
    rows = SEQ // GRID_W
    pos = np.stack([np.repeat(np.arange(rows), GRID_W), np.tile(np.arange(GRID_W), rows)], axis=-1)
    expo = (-np.arange(0, ROPE_AXIS, 2, dtype=np.float32) / np.float32(ROPE_AXIS)).astype(np.float64)
    ang = pos[:, :, None].astype(np.float64) * (ROPE_THETA ** expo)
    cos = np.stack([np.cos(ang), np.cos(ang)], axis=2).reshape(SEQ, QK_ROPE)
    sin = np.stack([-np.sin(ang), np.sin(ang)], axis=2).reshape(SEQ, QK_ROPE)
    cos = np.concatenate([np.ones((CTX_LEN, QK_ROPE)), cos], axis=0)
    sin = np.concatenate([np.zeros((CTX_LEN, QK_ROPE)), sin], axis=0)
    pad = np.zeros((TS, QK_ROPE))
    return (np.concatenate([cos, pad], axis=1).astype(np.float32),
            np.concatenate([sin, pad], axis=1).astype(np.float32))


def _rot_last(a):
    qt = QK_ROPE // 4
    return jnp.concatenate(
        [a[..., qt:2 * qt], a[..., :qt], a[..., 3 * qt:], a[..., 2 * qt:3 * qt]], axis=-1)


def kernel(x, c, ctx, c_ctx, w_ada, b_ada, g_pre_mix, w_in, q_norm, w_uq, kv_norm, w_ukv, w_o_mla,
           w_pool, pool_scale, w_o_pool, w_out, g_post_mix, g_pre_ffn, w_ffn_gate, w_ffn_up,
           w_ffn_down, g_post_ffn):
    assert x.shape == (BATCH, SEQ, D_MODEL) and ctx.shape == (BATCH, CTX_LEN, D_MODEL)
    cvec = jnp.concatenate(
        [c, c_ctx[None, :], jnp.zeros((MOD_ROWS - BATCH - 1, D_MODEL), F32)], axis=0)
    mod_all = _ada_call(cvec, w_ada, b_ada).reshape(DEPTH, MOD_ROWS, N_MOD, D_MODEL)
    cos_t, sin_t = _rope_tables()

    kr_cols = w_in[:, :, OFF_KR:OFF_POOL]
    win_a = jnp.concatenate(
        [w_in[:, :, :OFF_POOL], _rot_last(kr_cols), w_in[:, :, OFF_POOL:OFF_GATE]], axis=-1).astype(BF16)
    wg = w_in[:, :, OFF_GATE:].astype(BF16)
    wq = w_uq.reshape(DEPTH, Q_LORA, N_HEADS, QK_NOPE + QK_ROPE)
    wqn = wq[..., :QK_NOPE].reshape(DEPTH, Q_LORA, N_HEADS * QK_NOPE)
    wq_rope = wq[..., QK_NOPE:]
    wqr = jnp.concatenate([wq_rope, _rot_last(wq_rope)], axis=-1).reshape(
        DEPTH, Q_LORA, N_HEADS * 2 * QK_ROPE)
    wqf, weff = _fold_call(wqn, wqr, w_ukv, w_o_mla)
    wpool, wopool, wout = w_pool.astype(BF16), w_o_pool.astype(BF16), w_out.astype(BF16)
    wgate, wup, wdown = w_ffn_gate.astype(BF16), w_ffn_up.astype(BF16), w_ffn_down.astype(BF16)
    vec = lambda a: a[:, None, :]
    gpre, gpost, qn, kvn = vec(g_pre_mix), vec(g_post_mix), vec(q_norm), vec(kv_norm)
    gpre_f, gpost_f, pscale = vec(g_pre_ffn), vec(g_post_ffn), vec(pool_scale)

    x_args = (ctx, x)
    for l in range(DEPTH):
        last = l == DEPTH - 1
        q, k, ct, u = _inproj_call(l, x_args, mod_all, gpre, win_a, qn, wqf, kvn, cos_t, sin_t)
        if last:
            (o,), o_ctx = _attn_call(q, k, ct, ctx_queries=False), None
        else:
            o, o_ctx = _attn_call(q, k, ct, ctx_queries=True)
        xs1 = _mix_call(l, x_args, mod_all, gpre, wg, o, o_ctx, weff, u, wpool, pscale, wopool,
                        wout, gpost)
        xs = _ffn_call(l, xs1, mod_all, gpre_f, wgate, wup, wdown, gpost_f, with_ctx=not last)
        x_args = (xs,)
    return xs
```

```python
import functools

import jax
import jax.numpy as jnp
import numpy as np
from jax import lax
from jax.experimental import pallas as pl
from jax.experimental.pallas import tpu as pltpu

D_MODEL = 1024
BATCH = 4
SEQ = 4096
DEPTH = 2
GRID_W = 64
CTX_LEN = 256
N_HEADS = 8
QK_NOPE = 128
QK_ROPE = 64
V_DIM = 128
Q_LORA = 256
KV_LORA = 128
POOL_W = 512
POOL_WINDOWS = (2, 4, 8, 16)
N_POOL_GROUPS = 4
POOL_GROUP = POOL_W // N_POOL_GROUPS
D_FF = -(-8 * D_MODEL // (3 * 256)) * 256
N_BRANCH = 2
N_MOD = 6
ROPE_THETA = 10000.0
ROPE_AXIS = QK_ROPE // 2
EPS = 1e-6
ATTN_SCALE = (QK_NOPE + QK_ROPE) ** -0.5
Q_SCALE = ATTN_SCALE * 1.4426950408889634

OFF_KV = Q_LORA
OFF_KR = OFF_KV + KV_LORA
OFF_POOL = OFF_KR + QK_ROPE
OFF_GATE = OFF_POOL + POOL_W

F32 = jnp.float32
BF16 = jnp.bfloat16

TM = 256
TS = CTX_LEN + SEQ
NT = TS // TM
NT_LAT = SEQ // TM
LAT0 = NT - NT_LAT
TQ = TM
TPS = 2
NQS = SEQ // (TQ * TPS)
HEAD_W = 2 * KV_LORA
KV_W = QK_NOPE + V_DIM
CT_ROWS = KV_LORA + 16
O_W = N_HEADS * KV_LORA
HALO = 8
MOD_ROWS = 8
ADA_BN = 1536
MAX_ACC_ROWS = 32
VMEM_LIMIT = 52 * 1024 * 1024

assert CTX_LEN == TM and SEQ % TM == 0 and HALO * 2 == max(POOL_WINDOWS)
assert BATCH + 1 <= MOD_ROWS and (N_MOD * D_MODEL) % ADA_BN == 0
assert QK_NOPE == KV_LORA == V_DIM and 2 * QK_ROPE == KV_LORA
assert NQS % 2 == 0


def _layer_spec(arr, l):
    nd = arr.ndim
    return pl.BlockSpec((1,) + arr.shape[1:], lambda *_: (l,) + (0,) * (nd - 1),
                        pipeline_mode=pl.Buffered(1))


def _params(n_grid):
    return pltpu.CompilerParams(dimension_semantics=("arbitrary",) * n_grid,
                                vmem_limit_bytes=VMEM_LIMIT)


def _rms(x, g):
    return x * lax.rsqrt(jnp.mean(x * x, axis=-1, keepdims=True) + EPS) * g


def _dot(a, b):
    return jnp.dot(a, b, preferred_element_type=F32)


def _dot_t(a, b):
    return lax.dot_general(a, b, (((1,), (1,)), ((), ())), preferred_element_type=F32)


def _ada_kernel(c_ref, w_ref, b_ref, o_ref):
    cv = c_ref[...]
    s = cv * jax.nn.sigmoid(cv)
    o_ref[0] = _dot(s.astype(BF16), w_ref[0].astype(BF16)) + b_ref[0]


def _ada_call(cvec, w_ada, b_ada):
    n_out = N_MOD * D_MODEL
    return pl.pallas_call(
        _ada_kernel,
        grid=(DEPTH, n_out // ADA_BN),
        in_specs=[
            pl.BlockSpec((MOD_ROWS, D_MODEL), lambda l, n: (0, 0)),
            pl.BlockSpec((1, D_MODEL, ADA_BN), lambda l, n: (l, 0, n)),
            pl.BlockSpec((1, 1, ADA_BN), lambda l, n: (l, 0, n)),
        ],
        out_specs=pl.BlockSpec((1, MOD_ROWS, ADA_BN), lambda l, n: (l, 0, n)),
        out_shape=jax.ShapeDtypeStruct((DEPTH, MOD_ROWS, n_out), F32),
        compiler_params=_params(2),
        name="ada_mod",
    )(cvec, w_ada, b_ada.reshape(DEPTH, 1, n_out))


def _mod_spec(l, off):
    return pl.BlockSpec((1, 1, N_MOD, D_MODEL),
                        lambda b, j: (l, jnp.where(j + off == 0, BATCH, b), 0, 0))


def _fold_kernel(wqn_ref, wqr_ref, wukv_ref, womla_ref, wqf_ref, weff_ref):
    for hd in range(N_HEADS):
        n0 = hd * QK_NOPE
        c0 = hd * KV_W
        q0 = hd * HEAD_W
        wuk = wukv_ref[0, :, c0:c0 + QK_NOPE].astype(BF16)
        wuv = wukv_ref[0, :, c0 + QK_NOPE:c0 + KV_W].astype(BF16)
        wqf_ref[0, :, q0:q0 + KV_LORA] = _dot_t(
            wqn_ref[0, :, n0:n0 + QK_NOPE].astype(BF16), wuk).astype(BF16)
        wqf_ref[0, :, q0 + KV_LORA:q0 + HEAD_W] = wqr_ref[0, :, n0:n0 + 2 * QK_ROPE].astype(BF16)
        weff_ref[0, hd * KV_LORA:(hd + 1) * KV_LORA, :] = _dot(
            wuv, womla_ref[0, hd * V_DIM:(hd + 1) * V_DIM, :].astype(BF16)).astype(BF16)


def _fold_call(wqn, wqr, w_ukv, w_o_mla):
    layer = lambda a: pl.BlockSpec((1,) + a.shape[1:], lambda l: (l, 0, 0))
    return pl.pallas_call(
        _fold_kernel,
        grid=(DEPTH,),
        in_specs=[layer(wqn), layer(wqr), layer(w_ukv), layer(w_o_mla)],
        out_specs=[pl.BlockSpec((1, Q_LORA, N_HEADS * HEAD_W), lambda l: (l, 0, 0)),
                   pl.BlockSpec((1, O_W, D_MODEL), lambda l: (l, 0, 0))],
        out_shape=[jax.ShapeDtypeStruct((DEPTH, Q_LORA, N_HEADS * HEAD_W), BF16),
                   jax.ShapeDtypeStruct((DEPTH, O_W, D_MODEL), BF16)],
        compiler_params=_params(1),
        name="fold",
    )(wqn, wqr, w_ukv, w_o_mla)


def _row_specs(separate, off):
    if separate:
        return [pl.BlockSpec((1, TM, D_MODEL), lambda b, j: (b, 0, 0)),
                pl.BlockSpec((1, TM, D_MODEL), lambda b, j: (b, jnp.maximum(j + off - LAT0, 0), 0))]
    return [pl.BlockSpec((1, TM, D_MODEL), lambda b, j: (b, j + off, 0))]


def _load_rows(x_refs, j):
    if len(x_refs) == 2:
        return jnp.where(j == 0, x_refs[0][0], x_refs[1][0])
    return x_refs[0][0]


def _rope(t, cos, sin):
    return t * cos + pltpu.roll(t, QK_ROPE, axis=1) * sin


def _inproj_kernel(*refs, n_x):
    x_refs = refs[:n_x]
    (mod_ref, gpre_ref, win_ref, qn_ref, wqf_ref, kvn_ref, cos_ref, sin_ref,
     q_ref, k_ref, ct_ref, u_ref) = refs[n_x:]
    x = _load_rows(x_refs, pl.program_id(1))
    m = mod_ref[0, 0]
    h = _rms(x, gpre_ref[0]) * (1.0 + m[0:1]) + m[1:2]
    z = _dot(h.astype(BF16), win_ref[0])
    u_ref[0] = z[:, OFF_POOL + QK_ROPE:]
    cos = cos_ref[...]
    sin = sin_ref[...]

    qn = _rms(z[:, :OFF_KV], qn_ref[0])
    qa = _dot(qn.astype(BF16), wqf_ref[0])
    for hd in range(N_HEADS):
        c0 = hd * HEAD_W
        q_ref[0, hd, :, :KV_LORA] = (qa[:, c0:c0 + KV_LORA] * Q_SCALE).astype(BF16)
        q_ref[0, hd, :, KV_LORA:] = (
            _rope(qa[:, c0 + KV_LORA:c0 + HEAD_W], cos, sin) * Q_SCALE).astype(BF16)

    cn = _rms(z[:, OFF_KV:OFF_KR], kvn_ref[0])
    k_ref[0, :, :KV_LORA] = cn.astype(BF16)
    k_ref[0, :, KV_LORA:] = _rope(z[:, OFF_KR:OFF_KR + 2 * QK_ROPE], cos, sin).astype(BF16)
    ct_ref[0, :KV_LORA] = cn.T.astype(BF16)
    ct_ref[0, KV_LORA:] = (
        lax.broadcasted_iota(jnp.int32, (CT_ROWS - KV_LORA, TM), 0) == 0).astype(BF16)


def _inproj_call(l, x_args, mod_all, gpre, win_a, qnorm, wqf, kvnorm, cos_t, sin_t):
    tile = lambda b, j: (b, j, 0)
    return pl.pallas_call(
        functools.partial(_inproj_kernel, n_x=len(x_args)),
        grid=(BATCH, NT),
        in_specs=[
            *_row_specs(len(x_args) == 2, 0),
            _mod_spec(l, 0),
            _layer_spec(gpre, l),
            _layer_spec(win_a, l),
            _layer_spec(qnorm, l),
            _layer_spec(wqf, l),
            _layer_spec(kvnorm, l),
            pl.BlockSpec((TM, 2 * QK_ROPE), lambda b, j: (j, 0)),
            pl.BlockSpec((TM, 2 * QK_ROPE), lambda b, j: (j, 0)),
        ],
        out_specs=[
            pl.BlockSpec((1, N_HEADS, TM, HEAD_W), lambda b, j: (b, 0, j, 0)),
            pl.BlockSpec((1, TM, HEAD_W), tile),
            pl.BlockSpec((1, CT_ROWS, TM), lambda b, j: (b, 0, j)),
            pl.BlockSpec((1, TM, POOL_W), tile),
        ],
        out_shape=[
            jax.ShapeDtypeStruct((BATCH, N_HEADS, TS, HEAD_W), BF16),
            jax.ShapeDtypeStruct((BATCH, TS, HEAD_W), BF16),
            jax.ShapeDtypeStruct((BATCH, CT_ROWS, TS), BF16),
            jax.ShapeDtypeStruct((BATCH, TS, POOL_W), F32),
        ],
        compiler_params=_params(2),
        name="inproj",
    )(*x_args, mod_all, gpre, win_a, qnorm, wqf, kvnorm, cos_t, sin_t)


def _col_max(s):
    m = s[:MAX_ACC_ROWS]
    for r in range(MAX_ACC_ROWS, s.shape[0], MAX_ACC_ROWS):
        m = jnp.maximum(m, s[r:r + MAX_ACC_ROWS])
    return jnp.max(m, axis=0, keepdims=True)


def _softmax_pc_t(s, m, ct):
    p = jnp.exp2(s - m)
    ot = _dot(ct, p.astype(BF16))
    return (ot[:KV_LORA] / ot[KV_LORA:KV_LORA + 1]).T


def _attn_kernel(*refs, ctx_queries):
    qn_refs, q0_refs, refs = refs[:TPS], refs[TPS:2 * TPS], refs[2 * TPS:]
    if ctx_queries:
        qc_ref, k_ref, ct_ref, o_ref, octx_ref, *scratch = refs
    else:
        k_ref, ct_ref, o_ref, *scratch = refs
    sa, sb = scratch[:TPS], scratch[TPS:2 * TPS]
    ma, mb = scratch[2 * TPS:3 * TPS], scratch[3 * TPS:]
    b, h, i = pl.program_id(0), pl.program_id(1), pl.program_id(2)

    def scores_stage(q_ref, s_ref, m_ref):
        s = _dot_t(k_ref[0], q_ref[0, 0])
        s_ref[...] = s
        m_ref[...] = _col_max(s)

    def output_stage(t, s_ref, m_ref):
        o_ref[0, t * TQ:(t + 1) * TQ, :] = _softmax_pc_t(
            s_ref[...], m_ref[...], ct_ref[0]).astype(BF16)

    def step(s_in, m_in, s_out, m_out):
        for t in range(TPS):
            scores_stage(qn_refs[t], s_out[t], m_out[t])
            output_stage(t, s_in[t], m_in[t])

    @pl.when(jnp.logical_and(jnp.logical_and(b == 0, h == 0), i == 0))
    def _():
        for t in range(TPS):
            scores_stage(q0_refs[t], sa[t], ma[t])

    if ctx_queries:
        @pl.when(i == 0)
        def _():
            sc = _dot_t(k_ref[0, :CTX_LEN], qc_ref[0, 0])
            octx_ref[0] = _softmax_pc_t(sc, _col_max(sc), ct_ref[0, :, :CTX_LEN]).astype(BF16)

    @pl.when(i % 2 == 0)
    def _():
        step(sa, ma, sb, mb)

    @pl.when(i % 2 == 1)
    def _():
        step(sb, mb, sa, ma)


def _next_step(b, h, i):
    f = jnp.minimum((b * N_HEADS + h) * NQS + i + 1, BATCH * N_HEADS * NQS - 1)
    return f // (N_HEADS * NQS), (f // NQS) % N_HEADS, f % NQS


def _attn_call(q, k, ct, ctx_queries):
    def next_q(t):
        def index(b, h, i):
            nb, nh, ng = _next_step(b, h, i)
            return (nb, nh, ng * TPS + t + LAT0, 0)
        return index

    q_tile = (1, 1, TQ, HEAD_W)
    in_specs = [pl.BlockSpec(q_tile, next_q(t)) for t in range(TPS)]
    in_specs += [pl.BlockSpec(q_tile, functools.partial(lambda t, b, h, i: (0, 0, LAT0 + t, 0), t))
                 for t in range(TPS)]
    args = [q] * (2 * TPS)
    out_specs = [pl.BlockSpec((1, TPS * TQ, KV_LORA), lambda b, h, i: (b, i, h))]
    out_shape = [jax.ShapeDtypeStruct((BATCH, SEQ, O_W), BF16)]
    if ctx_queries:
        in_specs.append(pl.BlockSpec(q_tile, lambda b, h, i: (b, h, 0, 0)))
        args.append(q)
        out_specs.append(pl.BlockSpec((1, CTX_LEN, KV_LORA), lambda b, h, i: (b, 0, h)))
        out_shape.append(jax.ShapeDtypeStruct((BATCH, CTX_LEN, O_W), BF16))
    in_specs += [
        pl.BlockSpec((1, TS, HEAD_W), lambda b, h, i: (_next_step(b, h, i)[0], 0, 0)),
        pl.BlockSpec((1, CT_ROWS, TS), lambda b, h, i: (b, 0, 0)),
    ]
    return pl.pallas_call(
        functools.partial(_attn_kernel, ctx_queries=ctx_queries),
        grid=(BATCH, N_HEADS, NQS),
        in_specs=in_specs,
        out_specs=out_specs,
        out_shape=out_shape,
        scratch_shapes=([pltpu.VMEM((TS, TQ), F32)] * (2 * TPS)
                        + [pltpu.VMEM((1, TQ), F32)] * (2 * TPS)),
        compiler_params=_params(3),
        name="attention",
    )(*args, k, ct)


def _mix_kernel(*refs, n_x, tile_off, with_ctx):
    x_refs = refs[:n_x]
    mod_ref, gpre_ref, wg_ref, o_ref, *rest = refs[n_x:]
    if with_ctx:
        octx_ref, *rest = rest
    (weff_ref, uc_ref, up_ref, un_ref, wpool_ref, pscale_ref, wopool_ref, wout_ref,
     gpost_ref, out_ref, ext_ref) = rest
    j = pl.program_id(1) + tile_off
    x = _load_rows(x_refs, j)
    m = mod_ref[0, 0]
    h = (_rms(x, gpre_ref[0]) * (1.0 + m[0:1]) + m[1:2]).astype(BF16)
    gates = jax.nn.sigmoid(_dot(h, wg_ref[0]))
    o = jnp.where(j == 0, octx_ref[0], o_ref[0]) if with_ctx else o_ref[0]
    att = _dot(o, weff_ref[0])

    is_ctx = j == 0
    first = jnp.logical_or(is_ctx, j == LAT0)
    last = jnp.logical_or(is_ctx, j == NT - 1)
    u = uc_ref[0]
    ext_ref[0:HALO] = jnp.where(first, 0.0, up_ref[0])
    ext_ref[HALO:HALO + TM] = u
    ext_ref[HALO + TM:] = jnp.where(last, 0.0, un_ref[0])
    seq_len = jnp.where(is_ctx, CTX_LEN, SEQ)
    pos = lax.broadcasted_iota(jnp.int32, (TM, 1), 0) + jnp.where(is_ctx, 0, j - LAT0) * TM
    ys = []
    for g, w in enumerate(POOL_WINDOWS):
        lanes = slice(g * POOL_GROUP, (g + 1) * POOL_GROUP)
        acc = ext_ref[HALO - w // 2:HALO - w // 2 + TM, lanes]
        for s in range(1 - w // 2, w - w // 2):
            acc = acc + ext_ref[HALO + s:HALO + s + TM, lanes]
        cnt = jnp.minimum(pos + (w - w // 2), seq_len) - jnp.maximum(pos - w // 2, 0)
        d = acc / cnt.astype(F32) - u[:, lanes]
        ys.append(_dot(d.astype(BF16), wpool_ref[0, g]))
    yp = jnp.concatenate(ys, axis=1) * pscale_ref[0]
    o_pool = _dot(yp.astype(BF16), wopool_ref[0])

    merged = gates[:, :D_MODEL] * att + gates[:, D_MODEL:] * o_pool
    y = _dot(merged.astype(BF16), wout_ref[0])
    out_ref[0] = x + m[2:3] * _rms(y, gpost_ref[0])


def _mix_call(l, x_args, mod_all, gpre, wg, o, o_ctx, weff, u, wpool, pscale, wopool, wout, gpost):
    with_ctx = o_ctx is not None
    n_t = NT if with_ctx else NT_LAT
    off = 0 if with_ctx else LAT0
    hb = TM // HALO
    o_specs = [pl.BlockSpec((1, TM, O_W), lambda b, j: (b, jnp.maximum(j + off - LAT0, 0), 0))]
    o_args = [o]
    if with_ctx:
        o_specs.append(pl.BlockSpec((1, CTX_LEN, O_W), lambda b, j: (b, 0, 0)))
        o_args.append(o_ctx)
    return pl.pallas_call(
        functools.partial(_mix_kernel, n_x=len(x_args), tile_off=off, with_ctx=with_ctx),
        grid=(BATCH, n_t),
        in_specs=[
            *_row_specs(len(x_args) == 2, off),
            _mod_spec(l, off),
            _layer_spec(gpre, l),
            _layer_spec(wg, l),
            *o_specs,
            _layer_spec(weff, l),
            pl.BlockSpec((1, TM, POOL_W), lambda b, j: (b, j + off, 0)),
            pl.BlockSpec((1, HALO, POOL_W),
                         lambda b, j: (b, jnp.maximum((j + off) * hb - 1, 0), 0)),
            pl.BlockSpec((1, HALO, POOL_W),
                         lambda b, j: (b, jnp.minimum((j + off + 1) * hb, TS // HALO - 1), 0)),
            _layer_spec(wpool, l),
            _layer_spec(pscale, l),
            _layer_spec(wopool, l),
            _layer_spec(wout, l),
            _layer_spec(gpost, l),
        ],
        out_specs=pl.BlockSpec((1, TM, D_MODEL), lambda b, j: (b, j, 0)),
        out_shape=jax.ShapeDtypeStruct((BATCH, n_t * TM, D_MODEL), F32),
        scratch_shapes=[pltpu.VMEM((TM + 2 * HALO, POOL_W), F32)],
        compiler_params=_params(2),
        name="mix",
    )(*x_args, mod_all, gpre, wg, *o_args, weff, u, u, u, wpool, pscale, wopool, wout, gpost)


def _ffn_kernel(x_ref, mod_ref, gpre_ref, wgate_ref, wup_ref, wdown_ref, gpost_ref, out_ref):
    x = x_ref[0]
    m = mod_ref[0, 0]
    h = (_rms(x, gpre_ref[0]) * (1.0 + m[3:4]) + m[4:5]).astype(BF16)
    a = _dot(h, wgate_ref[0])
    t = (a * jax.nn.sigmoid(a)) * _dot(h, wup_ref[0])
    y = _dot(t.astype(BF16), wdown_ref[0])
    out_ref[0] = x + m[5:6] * _rms(y, gpost_ref[0])


def _ffn_call(l, xs, mod_all, gpre, wgate, wup, wdown, gpost, with_ctx):
    n_t = NT if with_ctx else NT_LAT
    off = 0 if with_ctx else LAT0
    tile = lambda b, j: (b, j, 0)
    if not with_ctx:
        rows = 2 * TM
        return pl.pallas_call(
            _ffn_kernel,
            grid=(BATCH, SEQ // rows),
            in_specs=[
                pl.BlockSpec((1, rows, D_MODEL), tile),
                _mod_spec(l, off),
                _layer_spec(gpre, l),
                _layer_spec(wgate, l),
                _layer_spec(wup, l),
                _layer_spec(wdown, l),
                _layer_spec(gpost, l),
            ],
            out_specs=pl.BlockSpec((1, rows, D_MODEL), tile),
            out_shape=jax.ShapeDtypeStruct((BATCH, SEQ, D_MODEL), F32),
            compiler_params=_params(2),
            name="ffn",
        )(xs, mod_all, gpre, wgate, wup, wdown, gpost)
    return pl.pallas_call(
        _ffn_kernel,
        grid=(BATCH, n_t),
        in_specs=[
            pl.BlockSpec((1, TM, D_MODEL), tile),
            _mod_spec(l, off),
            _layer_spec(gpre, l),
            _layer_spec(wgate, l),
            _layer_spec(wup, l),
            _layer_spec(wdown, l),
            _layer_spec(gpost, l),
        ],
        out_specs=pl.BlockSpec((1, TM, D_MODEL), tile),
        out_shape=jax.ShapeDtypeStruct((BATCH, n_t * TM, D_MODEL), F32),
        compiler_params=_params(2),
        name="ffn",
    )(xs, mod_all, gpre, wgate, wup, wdown, gpost)


def _rope_tables():
    rows = SEQ // GRID_W
    pos = np.stack([np.repeat(np.arange(rows), GRID_W), np.tile(np.arange(GRID_W), rows)], axis=-1)
    expo = (-np.arange(0, ROPE_AXIS, 2, dtype=np.float32) / np.float32(ROPE_AXIS)).astype(np.float64)
    ang = pos[:, :, None].astype(np.float64) * (ROPE_THETA ** expo)
    cos = np.stack([np.cos(ang), np.cos(ang)], axis=2).reshape(SEQ, QK_ROPE)
    sin = np.stack([-np.sin(ang), np.sin(ang)], axis=2).reshape(SEQ, QK_ROPE)
    cos = np.concatenate([np.ones((CTX_LEN, QK_ROPE)), cos], axis=0)
    sin = np.concatenate([np.zeros((CTX_LEN, QK_ROPE)), sin], axis=0)
    pad = np.zeros((TS, QK_ROPE))
    return (np.concatenate([cos, pad], axis=1).astype(np.float32),
            np.concatenate([sin, pad], axis=1).astype(np.float32))


def _rot_last(a):
    qt = QK_ROPE // 4
    return jnp.concatenate(
        [a[..., qt:2 * qt], a[..., :qt], a[..., 3 * qt:], a[..., 2 * qt:3 * qt]], axis=-1)


def kernel(x, c, ctx, c_ctx, w_ada, b_ada, g_pre_mix, w_in, q_norm, w_uq, kv_norm, w_ukv, w_o_mla,
           w_pool, pool_scale, w_o_pool, w_out, g_post_mix, g_pre_ffn, w_ffn_gate, w_ffn_up,
           w_ffn_down, g_post_ffn):
    assert x.shape == (BATCH, SEQ, D_MODEL) and ctx.shape == (BATCH, CTX_LEN, D_MODEL)
    cvec = jnp.concatenate(
        [c, c_ctx[None, :], jnp.zeros((MOD_ROWS - BATCH - 1, D_MODEL), F32)], axis=0)
    mod_all = _ada_call(cvec, w_ada, b_ada).reshape(DEPTH, MOD_ROWS, N_MOD, D_MODEL)
    cos_t, sin_t = _rope_tables()

    kr_cols = w_in[:, :, OFF_KR:OFF_POOL]
    win_a = jnp.concatenate(
        [w_in[:, :, :OFF_POOL], _rot_last(kr_cols), w_in[:, :, OFF_POOL:OFF_GATE]], axis=-1).astype(BF16)
    wg = w_in[:, :, OFF_GATE:].astype(BF16)
    wq = w_uq.reshape(DEPTH, Q_LORA, N_HEADS, QK_NOPE + QK_ROPE)
    wqn = wq[..., :QK_NOPE].reshape(DEPTH, Q_LORA, N_HEADS * QK_NOPE)
    wq_rope = wq[..., QK_NOPE:]
    wqr = jnp.concatenate([wq_rope, _rot_last(wq_rope)], axis=-1).reshape(
        DEPTH, Q_LORA, N_HEADS * 2 * QK_ROPE)
    wqf, weff = _fold_call(wqn, wqr, w_ukv, w_o_mla)
    wpool, wopool, wout = w_pool.astype(BF16), w_o_pool.astype(BF16), w_out.astype(BF16)
    wgate, wup, wdown = w_ffn_gate.astype(BF16), w_ffn_up.astype(BF16), w_ffn_down.astype(BF16)
    vec = lambda a: a[:, None, :]
    gpre, gpost, qn, kvn = vec(g_pre_mix), vec(g_post_mix), vec(q_norm), vec(kv_norm)
    gpre_f, gpost_f, pscale = vec(g_pre_ffn), vec(g_post_ffn), vec(pool_scale)

    x_args = (ctx, x)
    for l in range(DEPTH):
        last = l == DEPTH - 1
        q, k, ct, u = _inproj_call(l, x_args, mod_all, gpre, win_a, qn, wqf, kvn, cos_t, sin_t)
        if last:
            (o,), o_ctx = _attn_call(q, k, ct, ctx_queries=False), None
        else:
            o, o_ctx = _attn_call(q, k, ct, ctx_queries=True)
        xs1 = _mix_call(l, x_args, mod_all, gpre, wg, o, o_ctx, weff, u, wpool, pscale, wopool,
                        wout, gpost)
        xs = _ffn_call(l, xs1, mod_all, gpre_f, wgate, wup, wdown, gpost_f, with_ctx=not last)
        x_args = (xs,)
    return xs
```

```python
import functools

import jax
import jax.numpy as jnp
import numpy as np
from jax import lax
from jax.experimental import pallas as pl
from jax.experimental.pallas import tpu as pltpu

D_MODEL = 1024
BATCH = 4
SEQ = 4096
DEPTH = 2
GRID_W = 64
CTX_LEN = 256
N_HEADS = 8
QK_NOPE = 128
QK_ROPE = 64
V_DIM = 128
Q_LORA = 256
KV_LORA = 128
POOL_W = 512
POOL_WINDOWS = (2, 4, 8, 16)
N_POOL_GROUPS = 4
POOL_GROUP = POOL_W // N_POOL_GROUPS
D_FF = -(-8 * D_MODEL // (3 * 256)) * 256
N_BRANCH = 2
N_MOD = 6
ROPE_THETA = 10000.0
ROPE_AXIS = QK_ROPE // 2
EPS = 1e-6
ATTN_SCALE = (QK_NOPE + QK_ROPE) ** -0.5
Q_SCALE = ATTN_SCALE * 1.4426950408889634

OFF_KV = Q_LORA
OFF_KR = OFF_KV + KV_LORA
OFF_POOL = OFF_KR + QK_ROPE
OFF_GATE = OFF_POOL + POOL_W

F32 = jnp.float32
BF16 = jnp.bfloat16

TM = 256
TS = CTX_LEN + SEQ
NT = TS // TM
NT_LAT = SEQ // TM
LAT0 = NT - NT_LAT
TQ = TM
TPS = 4
NQS = SEQ // (TQ * TPS)
HEAD_W = 2 * KV_LORA
KV_W = QK_NOPE + V_DIM
CT_ROWS = KV_LORA + 16
O_W = N_HEADS * KV_LORA
HALO = 8
MOD_ROWS = 8
ADA_BN = 1536
MAX_ACC_ROWS = 32
VMEM_LIMIT = 52 * 1024 * 1024

assert CTX_LEN == TM and SEQ % TM == 0 and HALO * 2 == max(POOL_WINDOWS)
assert BATCH + 1 <= MOD_ROWS and (N_MOD * D_MODEL) % ADA_BN == 0
assert QK_NOPE == KV_LORA == V_DIM and 2 * QK_ROPE == KV_LORA
assert NQS % 2 == 0


def _layer_spec(arr, l):
    nd = arr.ndim
    return pl.BlockSpec((1,) + arr.shape[1:], lambda *_: (l,) + (0,) * (nd - 1),
                        pipeline_mode=pl.Buffered(1))


def _params(n_grid):
    return pltpu.CompilerParams(dimension_semantics=("arbitrary",) * n_grid,
                                vmem_limit_bytes=VMEM_LIMIT)


def _rms(x, g):
    return x * lax.rsqrt(jnp.mean(x * x, axis=-1, keepdims=True) + EPS) * g


def _dot(a, b):
    return jnp.dot(a, b, preferred_element_type=F32)


def _dot_t(a, b):
    return lax.dot_general(a, b, (((1,), (1,)), ((), ())), preferred_element_type=F32)


def _ada_kernel(c_ref, w_ref, b_ref, o_ref):
    cv = c_ref[...]
    s = cv * jax.nn.sigmoid(cv)
    o_ref[0] = _dot(s.astype(BF16), w_ref[0].astype(BF16)) + b_ref[0]


def _ada_call(cvec, w_ada, b_ada):
    n_out = N_MOD * D_MODEL
    return pl.pallas_call(
        _ada_kernel,
        grid=(DEPTH, n_out // ADA_BN),
        in_specs=[
            pl.BlockSpec((MOD_ROWS, D_MODEL), lambda l, n: (0, 0)),
            pl.BlockSpec((1, D_MODEL, ADA_BN), lambda l, n: (l, 0, n)),
            pl.BlockSpec((1, 1, ADA_BN), lambda l, n: (l, 0, n)),
        ],
        out_specs=pl.BlockSpec((1, MOD_ROWS, ADA_BN), lambda l, n: (l, 0, n)),
        out_shape=jax.ShapeDtypeStruct((DEPTH, MOD_ROWS, n_out), F32),
        compiler_params=_params(2),
        name="ada_mod",
    )(cvec, w_ada, b_ada.reshape(DEPTH, 1, n_out))


def _mod_spec(l, off):
    return pl.BlockSpec((1, 1, N_MOD, D_MODEL),
                        lambda b, j: (l, jnp.where(j + off == 0, BATCH, b), 0, 0))


def _fold_kernel(wqn_ref, wqr_ref, wukv_ref, womla_ref, wqf_ref, weff_ref):
    for hd in range(N_HEADS):
        n0 = hd * QK_NOPE
        c0 = hd * KV_W
        q0 = hd * HEAD_W
        wuk = wukv_ref[0, :, c0:c0 + QK_NOPE].astype(BF16)
        wuv = wukv_ref[0, :, c0 + QK_NOPE:c0 + KV_W].astype(BF16)
        wqf_ref[0, :, q0:q0 + KV_LORA] = _dot_t(
            wqn_ref[0, :, n0:n0 + QK_NOPE].astype(BF16), wuk).astype(BF16)
        wqf_ref[0, :, q0 + KV_LORA:q0 + HEAD_W] = wqr_ref[0, :, n0:n0 + 2 * QK_ROPE].astype(BF16)
        weff_ref[0, hd * KV_LORA:(hd + 1) * KV_LORA, :] = _dot(
            wuv, womla_ref[0, hd * V_DIM:(hd + 1) * V_DIM, :].astype(BF16)).astype(BF16)


def _fold_call(wqn, wqr, w_ukv, w_o_mla):
    layer = lambda a: pl.BlockSpec((1,) + a.shape[1:], lambda l: (l, 0, 0))
    return pl.pallas_call(
        _fold_kernel,
        grid=(DEPTH,),
        in_specs=[layer(wqn), layer(wqr), layer(w_ukv), layer(w_o_mla)],
        out_specs=[pl.BlockSpec((1, Q_LORA, N_HEADS * HEAD_W), lambda l: (l, 0, 0)),
                   pl.BlockSpec((1, O_W, D_MODEL), lambda l: (l, 0, 0))],
        out_shape=[jax.ShapeDtypeStruct((DEPTH, Q_LORA, N_HEADS * HEAD_W), BF16),
                   jax.ShapeDtypeStruct((DEPTH, O_W, D_MODEL), BF16)],
        compiler_params=_params(1),
        name="fold",
    )(wqn, wqr, w_ukv, w_o_mla)


def _row_specs(separate, off):
    if separate:
        return [pl.BlockSpec((1, TM, D_MODEL), lambda b, j: (b, 0, 0)),
                pl.BlockSpec((1, TM, D_MODEL), lambda b, j: (b, jnp.maximum(j + off - LAT0, 0), 0))]
    return [pl.BlockSpec((1, TM, D_MODEL), lambda b, j: (b, j + off, 0))]


def _load_rows(x_refs, j):
    if len(x_refs) == 2:
        return jnp.where(j == 0, x_refs[0][0], x_refs[1][0])
    return x_refs[0][0]


def _rope(t, cos, sin):
    return t * cos + pltpu.roll(t, QK_ROPE, axis=1) * sin


def _inproj_kernel(*refs, n_x):
    x_refs = refs[:n_x]
    (mod_ref, gpre_ref, win_ref, qn_ref, wqf_ref, kvn_ref, cos_ref, sin_ref,
     q_ref, k_ref, ct_ref, u_ref) = refs[n_x:]
    x = _load_rows(x_refs, pl.program_id(1))
    m = mod_ref[0, 0]
    h = _rms(x, gpre_ref[0]) * (1.0 + m[0:1]) + m[1:2]
    z = _dot(h.astype(BF16), win_ref[0])
    u_ref[0] = z[:, OFF_POOL + QK_ROPE:]
    cos = cos_ref[...]
    sin = sin_ref[...]

    qn = _rms(z[:, :OFF_KV], qn_ref[0])
    qa = _dot(qn.astype(BF16), wqf_ref[0])
    for hd in range(N_HEADS):
        c0 = hd * HEAD_W
        q_ref[0, hd, :, :KV_LORA] = (qa[:, c0:c0 + KV_LORA] * Q_SCALE).astype(BF16)
        q_ref[0, hd, :, KV_LORA:] = (
            _rope(qa[:, c0 + KV_LORA:c0 + HEAD_W], cos, sin) * Q_SCALE).astype(BF16)

    cn = _rms(z[:, OFF_KV:OFF_KR], kvn_ref[0])
    k_ref[0, :, :KV_LORA] = cn.astype(BF16)
    k_ref[0, :, KV_LORA:] = _rope(z[:, OFF_KR:OFF_KR + 2 * QK_ROPE], cos, sin).astype(BF16)
    ct_ref[0, :KV_LORA] = cn.T.astype(BF16)
    ct_ref[0, KV_LORA:] = (
        lax.broadcasted_iota(jnp.int32, (CT_ROWS - KV_LORA, TM), 0) == 0).astype(BF16)


def _inproj_call(l, x_args, mod_all, gpre, win_a, qnorm, wqf, kvnorm, cos_t, sin_t):
    tile = lambda b, j: (b, j, 0)
    return pl.pallas_call(
        functools.partial(_inproj_kernel, n_x=len(x_args)),
        grid=(BATCH, NT),
        in_specs=[
            *_row_specs(len(x_args) == 2, 0),
            _mod_spec(l, 0),
            _layer_spec(gpre, l),
            _layer_spec(win_a, l),
            _layer_spec(qnorm, l),
            _layer_spec(wqf, l),
            _layer_spec(kvnorm, l),
            pl.BlockSpec((TM, 2 * QK_ROPE), lambda b, j: (j, 0)),
            pl.BlockSpec((TM, 2 * QK_ROPE), lambda b, j: (j, 0)),
        ],
        out_specs=[
            pl.BlockSpec((1, N_HEADS, TM, HEAD_W), lambda b, j: (b, 0, j, 0)),
            pl.BlockSpec((1, TM, HEAD_W), tile),
            pl.BlockSpec((1, CT_ROWS, TM), lambda b, j: (b, 0, j)),
            pl.BlockSpec((1, TM, POOL_W), tile),
        ],
        out_shape=[
            jax.ShapeDtypeStruct((BATCH, N_HEADS, TS, HEAD_W), BF16),
            jax.ShapeDtypeStruct((BATCH, TS, HEAD_W), BF16),
            jax.ShapeDtypeStruct((BATCH, CT_ROWS, TS), BF16),
            jax.ShapeDtypeStruct((BATCH, TS, POOL_W), F32),
        ],
        compiler_params=_params(2),
        name="inproj",
    )(*x_args, mod_all, gpre, win_a, qnorm, wqf, kvnorm, cos_t, sin_t)


def _col_max(s):
    m = s[:MAX_ACC_ROWS]
    for r in range(MAX_ACC_ROWS, s.shape[0], MAX_ACC_ROWS):
        m = jnp.maximum(m, s[r:r + MAX_ACC_ROWS])
    return jnp.max(m, axis=0, keepdims=True)


def _softmax_pc_t(s, m, ct):
    p = jnp.exp2(s - m)
    ot = _dot(ct, p.astype(BF16))
    return (ot[:KV_LORA] / ot[KV_LORA:KV_LORA + 1]).T


def _attn_kernel(*refs, ctx_queries):
    qn_refs, q0_refs, refs = refs[:TPS], refs[TPS:2 * TPS], refs[2 * TPS:]
    if ctx_queries:
        qc_ref, k_ref, ct_ref, o_ref, octx_ref, *scratch = refs
    else:
        k_ref, ct_ref, o_ref, *scratch = refs
    sa, sb = scratch[:TPS], scratch[TPS:2 * TPS]
    ma, mb = scratch[2 * TPS:3 * TPS], scratch[3 * TPS:]
    b, h, i = pl.program_id(0), pl.program_id(1), pl.program_id(2)

    def scores_stage(q_ref, s_ref, m_ref):
        s = _dot_t(k_ref[0], q_ref[0, 0])
        s_ref[...] = s
        m_ref[...] = _col_max(s)

    def output_stage(t, s_ref, m_ref):
        o_ref[0, t * TQ:(t + 1) * TQ, :] = _softmax_pc_t(
            s_ref[...], m_ref[...], ct_ref[0]).astype(BF16)

    def step(s_in, m_in, s_out, m_out):
        for t in range(TPS):
            scores_stage(qn_refs[t], s_out[t], m_out[t])
            output_stage(t, s_in[t], m_in[t])

    @pl.when(jnp.logical_and(jnp.logical_and(b == 0, h == 0), i == 0))
    def _():
        for t in range(TPS):
            scores_stage(q0_refs[t], sa[t], ma[t])

    if ctx_queries:
        @pl.when(i == 0)
        def _():
            sc = _dot_t(k_ref[0, :CTX_LEN], qc_ref[0, 0])
            octx_ref[0] = _softmax_pc_t(sc, _col_max(sc), ct_ref[0, :, :CTX_LEN]).astype(BF16)

    @pl.when(i % 2 == 0)
    def _():
        step(sa, ma, sb, mb)

    @pl.when(i % 2 == 1)
    def _():
        step(sb, mb, sa, ma)


def _next_step(b, h, i):
    f = jnp.minimum((b * N_HEADS + h) * NQS + i + 1, BATCH * N_HEADS * NQS - 1)
    return f // (N_HEADS * NQS), (f // NQS) % N_HEADS, f % NQS


def _attn_call(q, k, ct, ctx_queries):
    def next_q(t):
        def index(b, h, i):
            nb, nh, ng = _next_step(b, h, i)
            return (nb, nh, ng * TPS + t + LAT0, 0)
        return index

    q_tile = (1, 1, TQ, HEAD_W)
    in_specs = [pl.BlockSpec(q_tile, next_q(t)) for t in range(TPS)]
    in_specs += [pl.BlockSpec(q_tile, functools.partial(lambda t, b, h, i: (0, 0, LAT0 + t, 0), t))
                 for t in range(TPS)]
    args = [q] * (2 * TPS)
    out_specs = [pl.BlockSpec((1, TPS * TQ, KV_LORA), lambda b, h, i: (b, i, h))]
    out_shape = [jax.ShapeDtypeStruct((BATCH, SEQ, O_W), BF16)]
    if ctx_queries:
        in_specs.append(pl.BlockSpec(q_tile, lambda b, h, i: (b, h, 0, 0)))
        args.append(q)
        out_specs.append(pl.BlockSpec((1, CTX_LEN, KV_LORA), lambda b, h, i: (b, 0, h)))
        out_shape.append(jax.ShapeDtypeStruct((BATCH, CTX_LEN, O_W), BF16))
    in_specs += [
        pl.BlockSpec((1, TS, HEAD_W), lambda b, h, i: (_next_step(b, h, i)[0], 0, 0)),
        pl.BlockSpec((1, CT_ROWS, TS), lambda b, h, i: (b, 0, 0)),
    ]
    return pl.pallas_call(
        functools.partial(_attn_kernel, ctx_queries=ctx_queries),
        grid=(BATCH, N_HEADS, NQS),
        in_specs=in_specs,
        out_specs=out_specs,
        out_shape=out_shape,
        scratch_shapes=([pltpu.VMEM((TS, TQ), F32)] * (2 * TPS)
                        + [pltpu.VMEM((1, TQ), F32)] * (2 * TPS)),
        compiler_params=_params(3),
        name="attention",
    )(*args, k, ct)


def _mix_kernel(*refs, n_x, tile_off, with_ctx):
    x_refs = refs[:n_x]
    mod_ref, gpre_ref, wg_ref, o_ref, *rest = refs[n_x:]
    if with_ctx:
        octx_ref, *rest = rest
    (weff_ref, uc_ref, up_ref, un_ref, wpool_ref, pscale_ref, wopool_ref, wout_ref,
     gpost_ref, out_ref, ext_ref) = rest
    j = pl.program_id(1) + tile_off
    x = _load_rows(x_refs, j)
    m = mod_ref[0, 0]
    h = (_rms(x, gpre_ref[0]) * (1.0 + m[0:1]) + m[1:2]).astype(BF16)
    gates = jax.nn.sigmoid(_dot(h, wg_ref[0]))
    o = jnp.where(j == 0, octx_ref[0], o_ref[0]) if with_ctx else o_ref[0]
    att = _dot(o, weff_ref[0])

    is_ctx = j == 0
    first = jnp.logical_or(is_ctx, j == LAT0)
    last = jnp.logical_or(is_ctx, j == NT - 1)
    u = uc_ref[0]
    ext_ref[0:HALO] = jnp.where(first, 0.0, up_ref[0])
    ext_ref[HALO:HALO + TM] = u
    ext_ref[HALO + TM:] = jnp.where(last, 0.0, un_ref[0])
    seq_len = jnp.where(is_ctx, CTX_LEN, SEQ)
    pos = lax.broadcasted_iota(jnp.int32, (TM, 1), 0) + jnp.where(is_ctx, 0, j - LAT0) * TM
    ys = []
    for g, w in enumerate(POOL_WINDOWS):
        lanes = slice(g * POOL_GROUP, (g + 1) * POOL_GROUP)
        acc = ext_ref[HALO - w // 2:HALO - w // 2 + TM, lanes]
        for s in range(1 - w // 2, w - w // 2):
            acc = acc + ext_ref[HALO + s:HALO + s + TM, lanes]
        cnt = jnp.minimum(pos + (w - w // 2), seq_len) - jnp.maximum(pos - w // 2, 0)
        d = acc / cnt.astype(F32) - u[:, lanes]
        ys.append(_dot(d.astype(BF16), wpool_ref[0, g]))
    yp = jnp.concatenate(ys, axis=1) * pscale_ref[0]
    o_pool = _dot(yp.astype(BF16), wopool_ref[0])

    merged = gates[:, :D_MODEL] * att + gates[:, D_MODEL:] * o_pool
    y = _dot(merged.astype(BF16), wout_ref[0])
    out_ref[0] = x + m[2:3] * _rms(y, gpost_ref[0])


def _mix_call(l, x_args, mod_all, gpre, wg, o, o_ctx, weff, u, wpool, pscale, wopool, wout, gpost):
    with_ctx = o_ctx is not None
    n_t = NT if with_ctx else NT_LAT
    off = 0 if with_ctx else LAT0
    hb = TM // HALO
    o_specs = [pl.BlockSpec((1, TM, O_W), lambda b, j: (b, jnp.maximum(j + off - LAT0, 0), 0))]
    o_args = [o]
    if with_ctx:
        o_specs.append(pl.BlockSpec((1, CTX_LEN, O_W), lambda b, j: (b, 0, 0)))
        o_args.append(o_ctx)
    return pl.pallas_call(
        functools.partial(_mix_kernel, n_x=len(x_args), tile_off=off, with_ctx=with_ctx),
        grid=(BATCH, n_t),
        in_specs=[
            *_row_specs(len(x_args) == 2, off),
            _mod_spec(l, off),
            _layer_spec(gpre, l),
            _layer_spec(wg, l),
            *o_specs,
            _layer_spec(weff, l),
            pl.BlockSpec((1, TM, POOL_W), lambda b, j: (b, j + off, 0)),
            pl.BlockSpec((1, HALO, POOL_W),
                         lambda b, j: (b, jnp.maximum((j + off) * hb - 1, 0), 0)),
            pl.BlockSpec((1, HALO, POOL_W),
                         lambda b, j: (b, jnp.minimum((j + off + 1) * hb, TS // HALO - 1), 0)),
            _layer_spec(wpool, l),
            _layer_spec(pscale, l),
            _layer_spec(wopool, l),
            _layer_spec(wout, l),
            _layer_spec(gpost, l),
        ],
        out_specs=pl.BlockSpec((1, TM, D_MODEL), lambda b, j: (b, j, 0)),
        out_shape=jax.ShapeDtypeStruct((BATCH, n_t * TM, D_MODEL), F32),
        scratch_shapes=[pltpu.VMEM((TM + 2 * HALO, POOL_W), F32)],
        compiler_params=_params(2),
        name="mix",
    )(*x_args, mod_all, gpre, wg, *o_args, weff, u, u, u, wpool, pscale, wopool, wout, gpost)


def _ffn_kernel(x_ref, mod_ref, gpre_ref, wgate_ref, wup_ref, wdown_ref, gpost_ref, out_ref):
    x = x_ref[0]
    m = mod_ref[0, 0]
    h = (_rms(x, gpre_ref[0]) * (1.0 + m[3:4]) + m[4:5]).astype(BF16)
    a = _dot(h, wgate_ref[0])
    t = (a * jax.nn.sigmoid(a)) * _dot(h, wup_ref[0])
    y = _dot(t.astype(BF16), wdown_ref[0])
    out_ref[0] = x + m[5:6] * _rms(y, gpost_ref[0])


def _ffn_call(l, xs, mod_all, gpre, wgate, wup, wdown, gpost, with_ctx):
    n_t = NT if with_ctx else NT_LAT
    off = 0 if with_ctx else LAT0
    tile = lambda b, j: (b, j, 0)
    if not with_ctx:
        rows = 2 * TM
        return pl.pallas_call(
            _ffn_kernel,
            grid=(BATCH, SEQ // rows),
            in_specs=[
                pl.BlockSpec((1, rows, D_MODEL), tile),
                _mod_spec(l, off),
                _layer_spec(gpre, l),
                _layer_spec(wgate, l),
                _layer_spec(wup, l),
                _layer_spec(wdown, l),
                _layer_spec(gpost, l),
            ],
            out_specs=pl.BlockSpec((1, rows, D_MODEL), tile),
            out_shape=jax.ShapeDtypeStruct((BATCH, SEQ, D_MODEL), F32),
            compiler_params=_params(2),
            name="ffn",
        )(xs, mod_all, gpre, wgate, wup, wdown, gpost)
    return pl.pallas_call(
        _ffn_kernel,
        grid=(BATCH, n_t),
        in_specs=[
            pl.BlockSpec((1, TM, D_MODEL), tile),
            _mod_spec(l, off),
            _layer_spec(gpre, l),
            _layer_spec(wgate, l),
            _layer_spec(wup, l),
            _layer_spec(wdown, l),
            _layer_spec(gpost, l),
        ],
        out_specs=pl.BlockSpec((1, TM, D_MODEL), tile),
        out_shape=jax.ShapeDtypeStruct((BATCH, n_t * TM, D_MODEL), F32),
        compiler_params=_params(2),
        name="ffn",
    )(xs, mod_all, gpre, wgate, wup, wdown, gpost)


def _rope_tables():
    rows = SEQ // GRID_W
    pos = np.stack([np.repeat(np.arange(rows), GRID_W), np.tile(np.arange(GRID_W), rows)], axis=-1)
    expo = (-np.arange(0, ROPE_AXIS, 2, dtype=np.float32) / np.float32(ROPE_AXIS)).astype(np.float64)
    ang = pos[:, :, None].astype(np.float64) * (ROPE_THETA ** expo)
    cos = np.stack([np.cos(ang), np.cos(ang)], axis=2).reshape(SEQ, QK_ROPE)
    sin = np.stack([-np.sin(ang), np.sin(ang)], axis=2).reshape(SEQ, QK_ROPE)
    cos = np.concatenate([np.ones((CTX_LEN, QK_ROPE)), cos], axis=0)
    sin = np.concatenate([np.zeros((CTX_LEN, QK_ROPE)), sin], axis=0)
    pad = np.zeros((TS, QK_ROPE))
    return (np.concatenate([cos, pad], axis=1).astype(np.float32),
            np.concatenate([sin, pad], axis=1).astype(np.float32))


def _rot_last(a):
    qt = QK_ROPE // 4
    return jnp.concatenate(
        [a[..., qt:2 * qt], a[..., :qt], a[..., 3 * qt:], a[..., 2 * qt:3 * qt]], axis=-1)


def kernel(x, c, ctx, c_ctx, w_ada, b_ada, g_pre_mix, w_in, q_norm, w_uq, kv_norm, w_ukv, w_o_mla,
           w_pool, pool_scale, w_o_pool, w_out, g_post_mix, g_pre_ffn, w_ffn_gate, w_ffn_up,
           w_ffn_down, g_post_ffn):
    assert x.shape == (BATCH, SEQ, D_MODEL) and ctx.shape == (BATCH, CTX_LEN, D_MODEL)
    cvec = jnp.concatenate(
        [c, c_ctx[None, :], jnp.zeros((MOD_ROWS - BATCH - 1, D_MODEL), F32)], axis=0)
    mod_all = _ada_call(cvec, w_ada, b_ada).reshape(DEPTH, MOD_ROWS, N_MOD, D_MODEL)
    cos_t, sin_t = _rope_tables()

    kr_cols = w_in[:, :, OFF_KR:OFF_POOL]
    win_a = jnp.concatenate(
        [w_in[:, :, :OFF_POOL], _rot_last(kr_cols), w_in[:, :, OFF_POOL:OFF_GATE]], axis=-1).astype(BF16)
    wg = w_in[:, :, OFF_GATE:].astype(BF16)
    wq = w_uq.reshape(DEPTH, Q_LORA, N_HEADS, QK_NOPE + QK_ROPE)
    wqn = wq[..., :QK_NOPE].reshape(DEPTH, Q_LORA, N_HEADS * QK_NOPE)
    wq_rope = wq[..., QK_NOPE:]
    wqr = jnp.concatenate([wq_rope, _rot_last(wq_rope)], axis=-1).reshape(
        DEPTH, Q_LORA, N_HEADS * 2 * QK_ROPE)
    wqf, weff = _fold_call(wqn, wqr, w_ukv, w_o_mla)
    wpool, wopool, wout = w_pool.astype(BF16), w_o_pool.astype(BF16), w_out.astype(BF16)
    wgate, wup, wdown = w_ffn_gate.astype(BF16), w_ffn_up.astype(BF16), w_ffn_down.astype(BF16)
    vec = lambda a: a[:, None, :]
    gpre, gpost, qn, kvn = vec(g_pre_mix), vec(g_post_mix), vec(q_norm), vec(kv_norm)
    gpre_f, gpost_f, pscale = vec(g_pre_ffn), vec(g_post_ffn), vec(pool_scale)

    x_args = (ctx, x)
    for l in range(DEPTH):
        last = l == DEPTH - 1
        q, k, ct, u = _inproj_call(l, x_args, mod_all, gpre, win_a, qn, wqf, kvn, cos_t, sin_t)
        if last:
            (o,), o_ctx = _attn_call(q, k, ct, ctx_queries=False), None
        else:
            o, o_ctx = _attn_call(q, k, ct, ctx_queries=True)
        xs1 = _mix_call(l, x_args, mod_all, gpre, wg, o, o_ctx, weff, u, wpool, pscale, wopool,
                        wout, gpost)
        xs = _ffn_call(l, xs1, mod_all, gpre_f, wgate, wup, wdown, gpost_f, with_ctx=not last)
        x_args = (xs,)
    return xs
```

```python
import functools

import jax
import jax.numpy as jnp
import numpy as np
from jax import lax
from jax.experimental import pallas as pl
from jax.experimental.pallas import tpu as pltpu

D_MODEL = 1024
BATCH = 4
SEQ = 4096
DEPTH = 2
GRID_W = 64
CTX_LEN = 256
N_HEADS = 8
QK_NOPE = 128
QK_ROPE = 64
V_DIM = 128
Q_LORA = 256
KV_LORA = 128
POOL_W = 512
POOL_WINDOWS = (2, 4, 8, 16)
N_POOL_GROUPS = 4
POOL_GROUP = POOL_W // N_POOL_GROUPS
D_FF = -(-8 * D_MODEL // (3 * 256)) * 256
N_BRANCH = 2
N_MOD = 6
ROPE_THETA = 10000.0
ROPE_AXIS = QK_ROPE // 2
EPS = 1e-6
ATTN_SCALE = (QK_NOPE + QK_ROPE) ** -0.5
Q_SCALE = ATTN_SCALE * 1.4426950408889634

OFF_KV = Q_LORA
OFF_KR = OFF_KV + KV_LORA
OFF_POOL = OFF_KR + QK_ROPE
OFF_GATE = OFF_POOL + POOL_W

F32 = jnp.float32
BF16 = jnp.bfloat16

TM = 256
TS = CTX_LEN + SEQ
NT = TS // TM
NT_LAT = SEQ // TM
LAT0 = NT - NT_LAT
TQ = TM
TPS = 4
MIX_TILES = 2
NQS = SEQ // (TQ * TPS)
HEAD_W = 2 * KV_LORA
KV_W = QK_NOPE + V_DIM
CT_ROWS = KV_LORA + 16
O_W = N_HEADS * KV_LORA
HALO = 8
MOD_ROWS = 8
ADA_BN = 1536
MAX_ACC_ROWS = 32
VMEM_LIMIT = 52 * 1024 * 1024

assert CTX_LEN == TM and SEQ % TM == 0 and HALO * 2 == max(POOL_WINDOWS)
assert BATCH + 1 <= MOD_ROWS and (N_MOD * D_MODEL) % ADA_BN == 0
assert QK_NOPE == KV_LORA == V_DIM and 2 * QK_ROPE == KV_LORA
assert NQS % 2 == 0


def _layer_spec(arr, l):
    nd = arr.ndim
    return pl.BlockSpec((1,) + arr.shape[1:], lambda *_: (l,) + (0,) * (nd - 1),
                        pipeline_mode=pl.Buffered(1))


def _params(n_grid):
    return pltpu.CompilerParams(dimension_semantics=("arbitrary",) * n_grid,
                                vmem_limit_bytes=VMEM_LIMIT)


def _rms(x, g):
    return x * lax.rsqrt(jnp.mean(x * x, axis=-1, keepdims=True) + EPS) * g


def _dot(a, b):
    return jnp.dot(a, b, preferred_element_type=F32)


def _dot_t(a, b):
    return lax.dot_general(a, b, (((1,), (1,)), ((), ())), preferred_element_type=F32)


def _ada_kernel(c_ref, w_ref, b_ref, o_ref):
    cv = c_ref[...]
    s = cv * jax.nn.sigmoid(cv)
    o_ref[0] = _dot(s.astype(BF16), w_ref[0].astype(BF16)) + b_ref[0]


def _ada_call(cvec, w_ada, b_ada):
    n_out = N_MOD * D_MODEL
    return pl.pallas_call(
        _ada_kernel,
        grid=(DEPTH, n_out // ADA_BN),
        in_specs=[
            pl.BlockSpec((MOD_ROWS, D_MODEL), lambda l, n: (0, 0)),
            pl.BlockSpec((1, D_MODEL, ADA_BN), lambda l, n: (l, 0, n)),
            pl.BlockSpec((1, 1, ADA_BN), lambda l, n: (l, 0, n)),
        ],
        out_specs=pl.BlockSpec((1, MOD_ROWS, ADA_BN), lambda l, n: (l, 0, n)),
        out_shape=jax.ShapeDtypeStruct((DEPTH, MOD_ROWS, n_out), F32),
        compiler_params=_params(2),
        name="ada_mod",
    )(cvec, w_ada, b_ada.reshape(DEPTH, 1, n_out))


def _mod_spec(l, off):
    return pl.BlockSpec((1, 1, N_MOD, D_MODEL),
                        lambda b, j: (l, jnp.where(j + off == 0, BATCH, b), 0, 0))


def _fold_kernel(wqn_ref, wqr_ref, wukv_ref, womla_ref, wqf_ref, weff_ref):
    for hd in range(N_HEADS):
        n0 = hd * QK_NOPE
        c0 = hd * KV_W
        q0 = hd * HEAD_W
        wuk = wukv_ref[0, :, c0:c0 + QK_NOPE].astype(BF16)
        wuv = wukv_ref[0, :, c0 + QK_NOPE:c0 + KV_W].astype(BF16)
        wqf_ref[0, :, q0:q0 + KV_LORA] = _dot_t(
            wqn_ref[0, :, n0:n0 + QK_NOPE].astype(BF16), wuk).astype(BF16)
        wqf_ref[0, :, q0 + KV_LORA:q0 + HEAD_W] = wqr_ref[0, :, n0:n0 + 2 * QK_ROPE].astype(BF16)
        weff_ref[0, hd * KV_LORA:(hd + 1) * KV_LORA, :] = _dot(
            wuv, womla_ref[0, hd * V_DIM:(hd + 1) * V_DIM, :].astype(BF16)).astype(BF16)


def _fold_call(wqn, wqr, w_ukv, w_o_mla):
    layer = lambda a: pl.BlockSpec((1,) + a.shape[1:], lambda l: (l, 0, 0))
    return pl.pallas_call(
        _fold_kernel,
        grid=(DEPTH,),
        in_specs=[layer(wqn), layer(wqr), layer(w_ukv), layer(w_o_mla)],
        out_specs=[pl.BlockSpec((1, Q_LORA, N_HEADS * HEAD_W), lambda l: (l, 0, 0)),
                   pl.BlockSpec((1, O_W, D_MODEL), lambda l: (l, 0, 0))],
        out_shape=[jax.ShapeDtypeStruct((DEPTH, Q_LORA, N_HEADS * HEAD_W), BF16),
                   jax.ShapeDtypeStruct((DEPTH, O_W, D_MODEL), BF16)],
        compiler_params=_params(1),
        name="fold",
    )(wqn, wqr, w_ukv, w_o_mla)


def _row_specs(separate, off):
    if separate:
        return [pl.BlockSpec((1, TM, D_MODEL), lambda b, j: (b, 0, 0)),
                pl.BlockSpec((1, TM, D_MODEL), lambda b, j: (b, jnp.maximum(j + off - LAT0, 0), 0))]
    return [pl.BlockSpec((1, TM, D_MODEL), lambda b, j: (b, j + off, 0))]


def _load_rows(x_refs, j):
    if len(x_refs) == 2:
        return jnp.where(j == 0, x_refs[0][0], x_refs[1][0])
    return x_refs[0][0]


def _rope(t, cos, sin):
    return t * cos + pltpu.roll(t, QK_ROPE, axis=1) * sin


def _inproj_kernel(*refs, n_x):
    x_refs = refs[:n_x]
    (mod_ref, gpre_ref, win_ref, qn_ref, wqf_ref, kvn_ref, cos_ref, sin_ref,
     q_ref, k_ref, ct_ref, u_ref) = refs[n_x:]
    x = _load_rows(x_refs, pl.program_id(1))
    m = mod_ref[0, 0]
    h = _rms(x, gpre_ref[0]) * (1.0 + m[0:1]) + m[1:2]
    z = _dot(h.astype(BF16), win_ref[0])
    u_ref[0] = z[:, OFF_POOL + QK_ROPE:]
    cos = cos_ref[...]
    sin = sin_ref[...]

    qn = _rms(z[:, :OFF_KV], qn_ref[0])
    qa = _dot(qn.astype(BF16), wqf_ref[0])
    for hd in range(N_HEADS):
        c0 = hd * HEAD_W
        q_ref[0, hd, :, :KV_LORA] = (qa[:, c0:c0 + KV_LORA] * Q_SCALE).astype(BF16)
        q_ref[0, hd, :, KV_LORA:] = (
            _rope(qa[:, c0 + KV_LORA:c0 + HEAD_W], cos, sin) * Q_SCALE).astype(BF16)

    cn = _rms(z[:, OFF_KV:OFF_KR], kvn_ref[0])
    k_ref[0, :, :KV_LORA] = cn.astype(BF16)
    k_ref[0, :, KV_LORA:] = _rope(z[:, OFF_KR:OFF_KR + 2 * QK_ROPE], cos, sin).astype(BF16)
    ct_ref[0, :KV_LORA] = cn.T.astype(BF16)
    ct_ref[0, KV_LORA:] = (
        lax.broadcasted_iota(jnp.int32, (CT_ROWS - KV_LORA, TM), 0) == 0).astype(BF16)


def _inproj_call(l, x_args, mod_all, gpre, win_a, qnorm, wqf, kvnorm, cos_t, sin_t):
    tile = lambda b, j: (b, j, 0)
    return pl.pallas_call(
        functools.partial(_inproj_kernel, n_x=len(x_args)),
        grid=(BATCH, NT),
        in_specs=[
            *_row_specs(len(x_args) == 2, 0),
            _mod_spec(l, 0),
            _layer_spec(gpre, l),
            _layer_spec(win_a, l),
            _layer_spec(qnorm, l),
            _layer_spec(wqf, l),
            _layer_spec(kvnorm, l),
            pl.BlockSpec((TM, 2 * QK_ROPE), lambda b, j: (j, 0)),
            pl.BlockSpec((TM, 2 * QK_ROPE), lambda b, j: (j, 0)),
        ],
        out_specs=[
            pl.BlockSpec((1, N_HEADS, TM, HEAD_W), lambda b, j: (b, 0, j, 0)),
            pl.BlockSpec((1, TM, HEAD_W), tile),
            pl.BlockSpec((1, CT_ROWS, TM), lambda b, j: (b, 0, j)),
            pl.BlockSpec((1, TM, POOL_W), tile),
        ],
        out_shape=[
            jax.ShapeDtypeStruct((BATCH, N_HEADS, TS, HEAD_W), BF16),
            jax.ShapeDtypeStruct((BATCH, TS, HEAD_W), BF16),
            jax.ShapeDtypeStruct((BATCH, CT_ROWS, TS), BF16),
            jax.ShapeDtypeStruct((BATCH, TS, POOL_W), F32),
        ],
        compiler_params=_params(2),
        name="inproj",
    )(*x_args, mod_all, gpre, win_a, qnorm, wqf, kvnorm, cos_t, sin_t)


def _col_max(s):
    m = s[:MAX_ACC_ROWS]
    for r in range(MAX_ACC_ROWS, s.shape[0], MAX_ACC_ROWS):
        m = jnp.maximum(m, s[r:r + MAX_ACC_ROWS])
    return jnp.max(m, axis=0, keepdims=True)


def _softmax_pc_t(s, m, ct):
    p = jnp.exp2(s - m)
    ot = _dot(ct, p.astype(BF16))
    return (ot[:KV_LORA] / ot[KV_LORA:KV_LORA + 1]).T


def _attn_kernel(*refs, ctx_queries):
    qn_refs, q0_refs, refs = refs[:TPS], refs[TPS:2 * TPS], refs[2 * TPS:]
    if ctx_queries:
        qc_ref, k_ref, ct_ref, o_ref, octx_ref, *scratch = refs
    else:
        k_ref, ct_ref, o_ref, *scratch = refs
    sa, sb = scratch[:TPS], scratch[TPS:2 * TPS]
    ma, mb = scratch[2 * TPS:3 * TPS], scratch[3 * TPS:]
    b, h, i = pl.program_id(0), pl.program_id(1), pl.program_id(2)

    def scores_stage(q_ref, s_ref, m_ref):
        s = _dot_t(k_ref[0], q_ref[0, 0])
        s_ref[...] = s
        m_ref[...] = _col_max(s)

    def output_stage(t, s_ref, m_ref):
        o_ref[0, t * TQ:(t + 1) * TQ, :] = _softmax_pc_t(
            s_ref[...], m_ref[...], ct_ref[0]).astype(BF16)

    def step(s_in, m_in, s_out, m_out):
        for t in range(TPS):
            scores_stage(qn_refs[t], s_out[t], m_out[t])
            output_stage(t, s_in[t], m_in[t])

    @pl.when(jnp.logical_and(jnp.logical_and(b == 0, h == 0), i == 0))
    def _():
        for t in range(TPS):
            scores_stage(q0_refs[t], sa[t], ma[t])

    if ctx_queries:
        @pl.when(i == 0)
        def _():
            sc = _dot_t(k_ref[0, :CTX_LEN], qc_ref[0, 0])
            octx_ref[0] = _softmax_pc_t(sc, _col_max(sc), ct_ref[0, :, :CTX_LEN]).astype(BF16)

    @pl.when(i % 2 == 0)
    def _():
        step(sa, ma, sb, mb)

    @pl.when(i % 2 == 1)
    def _():
        step(sb, mb, sa, ma)


def _next_step(b, h, i):
    f = jnp.minimum((b * N_HEADS + h) * NQS + i + 1, BATCH * N_HEADS * NQS - 1)
    return f // (N_HEADS * NQS), (f // NQS) % N_HEADS, f % NQS


def _attn_call(q, k, ct, ctx_queries):
    def next_q(t):
        def index(b, h, i):
            nb, nh, ng = _next_step(b, h, i)
            return (nb, nh, ng * TPS + t + LAT0, 0)
        return index

    q_tile = (1, 1, TQ, HEAD_W)
    in_specs = [pl.BlockSpec(q_tile, next_q(t)) for t in range(TPS)]
    in_specs += [pl.BlockSpec(q_tile, functools.partial(lambda t, b, h, i: (0, 0, LAT0 + t, 0), t))
                 for t in range(TPS)]
    args = [q] * (2 * TPS)
    out_specs = [pl.BlockSpec((1, TPS * TQ, KV_LORA), lambda b, h, i: (b, i, h))]
    out_shape = [jax.ShapeDtypeStruct((BATCH, SEQ, O_W), BF16)]
    if ctx_queries:
        in_specs.append(pl.BlockSpec(q_tile, lambda b, h, i: (b, h, 0, 0)))
        args.append(q)
        out_specs.append(pl.BlockSpec((1, CTX_LEN, KV_LORA), lambda b, h, i: (b, 0, h)))
        out_shape.append(jax.ShapeDtypeStruct((BATCH, CTX_LEN, O_W), BF16))
    in_specs += [
        pl.BlockSpec((1, TS, HEAD_W), lambda b, h, i: (_next_step(b, h, i)[0], 0, 0)),
        pl.BlockSpec((1, CT_ROWS, TS), lambda b, h, i: (b, 0, 0)),
    ]
    return pl.pallas_call(
        functools.partial(_attn_kernel, ctx_queries=ctx_queries),
        grid=(BATCH, N_HEADS, NQS),
        in_specs=in_specs,
        out_specs=out_specs,
        out_shape=out_shape,
        scratch_shapes=([pltpu.VMEM((TS, TQ), F32)] * (2 * TPS)
                        + [pltpu.VMEM((1, TQ), F32)] * (2 * TPS)),
        compiler_params=_params(3),
        name="attention",
    )(*args, k, ct)


def _sigmoid(z):
    return 0.5 * (jnp.tanh(0.5 * z) + 1.0)


def _mix_kernel(*refs, n_x, n_t, tile_off, with_ctx):
    n_tile = n_x + 5 + int(with_ctx)
    (gpre_ref, wg_ref, weff_ref, wpool_ref, pscale_ref, wopool_ref, wout_ref, gpost_ref,
     out_ref, *ext_refs) = refs[MIX_TILES * n_tile:]
    for t in range(MIX_TILES):
        x_refs = refs[t * n_tile:t * n_tile + n_x]
        mod_ref, o_ref, *rest = refs[t * n_tile + n_x:(t + 1) * n_tile]
        if with_ctx:
            octx_ref, *rest = rest
        uc_ref, up_ref, un_ref = rest
        ext_ref = ext_refs[t]
        j = (pl.program_id(0) * MIX_TILES + t) % n_t + tile_off
        x = _load_rows(x_refs, j)
        m = mod_ref[0, 0]
        h = (_rms(x, gpre_ref[0]) * (1.0 + m[0:1]) + m[1:2]).astype(BF16)
        gates = _sigmoid(_dot(h, wg_ref[0]))
        o = jnp.where(j == 0, octx_ref[0], o_ref[0]) if with_ctx else o_ref[0]
        att = _dot(o, weff_ref[0])

        is_ctx = j == 0
        first = jnp.logical_or(is_ctx, j == LAT0)
        last = jnp.logical_or(is_ctx, j == NT - 1)
        u = uc_ref[0]
        ext_ref[0:HALO] = jnp.where(first, 0.0, up_ref[0])
        ext_ref[HALO:HALO + TM] = u
        ext_ref[HALO + TM:] = jnp.where(last, 0.0, un_ref[0])
        seq_len = jnp.where(is_ctx, CTX_LEN, SEQ)
        pos = lax.broadcasted_iota(jnp.int32, (TM, 1), 0) + jnp.where(is_ctx, 0, j - LAT0) * TM
        ys = []
        for g, w in enumerate(POOL_WINDOWS):
            lanes = slice(g * POOL_GROUP, (g + 1) * POOL_GROUP)
            acc = ext_ref[HALO - w // 2:HALO - w // 2 + TM, lanes]
            for s in range(1 - w // 2, w - w // 2):
                acc = acc + ext_ref[HALO + s:HALO + s + TM, lanes]
            cnt = jnp.minimum(pos + (w - w // 2), seq_len) - jnp.maximum(pos - w // 2, 0)
            d = acc / cnt.astype(F32) - u[:, lanes]
            ys.append(_dot(d.astype(BF16), wpool_ref[0, g]))
        yp = jnp.concatenate(ys, axis=1) * pscale_ref[0]
        o_pool = _dot(yp.astype(BF16), wopool_ref[0])

        merged = gates[:, :D_MODEL] * att + gates[:, D_MODEL:] * o_pool
        y = _dot(merged.astype(BF16), wout_ref[0])
        out_ref[t * TM:(t + 1) * TM, :] = x + m[2:3] * _rms(y, gpost_ref[0])


def _mix_call(l, x_args, mod_all, gpre, wg, o, o_ctx, weff, u, wpool, pscale, wopool, wout, gpost):
    with_ctx = o_ctx is not None
    n_t = NT if with_ctx else NT_LAT
    off = 0 if with_ctx else LAT0
    hb = TM // HALO
    assert (BATCH * n_t) % MIX_TILES == 0

    def tile_specs(t):
        def at(fn):
            def index(p):
                f = p * MIX_TILES + t
                return fn(f // n_t, f % n_t + off)
            return index

        lat = lambda b, j: (b, jnp.maximum(j - LAT0, 0), 0)
        specs = ([pl.BlockSpec((1, TM, D_MODEL), at(lambda b, j: (b, 0, 0))),
                  pl.BlockSpec((1, TM, D_MODEL), at(lat))] if len(x_args) == 2
                 else [pl.BlockSpec((1, TM, D_MODEL), at(lambda b, j: (b, j, 0)))])
        specs.append(pl.BlockSpec((1, 1, N_MOD, D_MODEL),
                                  at(lambda b, j: (l, jnp.where(j == 0, BATCH, b), 0, 0))))
        specs.append(pl.BlockSpec((1, TM, O_W), at(lat)))
        if with_ctx:
            specs.append(pl.BlockSpec((1, CTX_LEN, O_W), at(lambda b, j: (b, 0, 0))))
        specs += [
            pl.BlockSpec((1, TM, POOL_W), at(lambda b, j: (b, j, 0))),
            pl.BlockSpec((1, HALO, POOL_W), at(lambda b, j: (b, jnp.maximum(j * hb - 1, 0), 0))),
            pl.BlockSpec((1, HALO, POOL_W),
                         at(lambda b, j: (b, jnp.minimum((j + 1) * hb, TS // HALO - 1), 0))),
        ]
        return specs

    tile_args = [*x_args, mod_all, o, *([o_ctx] if with_ctx else []), u, u, u]
    weights = [gpre, wg, weff, wpool, pscale, wopool, wout, gpost]
    out = pl.pallas_call(
        functools.partial(_mix_kernel, n_x=len(x_args), n_t=n_t, tile_off=off, with_ctx=with_ctx),
        grid=(BATCH * n_t // MIX_TILES,),
        in_specs=([s for t in range(MIX_TILES) for s in tile_specs(t)]
                  + [_layer_spec(w, l) for w in weights]),
        out_specs=pl.BlockSpec((MIX_TILES * TM, D_MODEL), lambda p: (p, 0)),
        out_shape=jax.ShapeDtypeStruct((BATCH * n_t * TM, D_MODEL), F32),
        scratch_shapes=[pltpu.VMEM((TM + 2 * HALO, POOL_W), F32)] * MIX_TILES,
        compiler_params=_params(1),
        name="mix",
    )(*(tile_args * MIX_TILES), *weights)
    return out.reshape(BATCH, n_t * TM, D_MODEL)


def _ffn_kernel(x_ref, mod_ref, gpre_ref, wgate_ref, wup_ref, wdown_ref, gpost_ref, out_ref):
    x = x_ref[0]
    m = mod_ref[0, 0]
    h = (_rms(x, gpre_ref[0]) * (1.0 + m[3:4]) + m[4:5]).astype(BF16)
    a = _dot(h, wgate_ref[0])
    t = (a * _sigmoid(a)) * _dot(h, wup_ref[0])
    y = _dot(t.astype(BF16), wdown_ref[0])
    out_ref[0] = x + m[5:6] * _rms(y, gpost_ref[0])


def _ffn_call(l, xs, mod_all, gpre, wgate, wup, wdown, gpost, with_ctx):
    n_t = NT if with_ctx else NT_LAT
    off = 0 if with_ctx else LAT0
    tile = lambda b, j: (b, j, 0)
    if not with_ctx:
        rows = 2 * TM
        return pl.pallas_call(
            _ffn_kernel,
            grid=(BATCH, SEQ // rows),
            in_specs=[
                pl.BlockSpec((1, rows, D_MODEL), tile),
                _mod_spec(l, off),
                _layer_spec(gpre, l),
                _layer_spec(wgate, l),
                _layer_spec(wup, l),
                _layer_spec(wdown, l),
                _layer_spec(gpost, l),
            ],
            out_specs=pl.BlockSpec((1, rows, D_MODEL), tile),
            out_shape=jax.ShapeDtypeStruct((BATCH, SEQ, D_MODEL), F32),
            compiler_params=_params(2),
            name="ffn",
        )(xs, mod_all, gpre, wgate, wup, wdown, gpost)
    return pl.pallas_call(
        _ffn_kernel,
        grid=(BATCH, n_t),
        in_specs=[
            pl.BlockSpec((1, TM, D_MODEL), tile),
            _mod_spec(l, off),
            _layer_spec(gpre, l),
            _layer_spec(wgate, l),
            _layer_spec(wup, l),
            _layer_spec(wdown, l),
            _layer_spec(gpost, l),
        ],
        out_specs=pl.BlockSpec((1, TM, D_MODEL), tile),
        out_shape=jax.ShapeDtypeStruct((BATCH, n_t * TM, D_MODEL), F32),
        compiler_params=_params(2),
        name="ffn",
    )(xs, mod_all, gpre, wgate, wup, wdown, gpost)


def _rope_tables():
    rows = SEQ // GRID_W
    pos = np.stack([np.repeat(np.arange(rows), GRID_W), np.tile(np.arange(GRID_W), rows)], axis=-1)
    expo = (-np.arange(0, ROPE_AXIS, 2, dtype=np.float32) / np.float32(ROPE_AXIS)).astype(np.float64)
    ang = pos[:, :, None].astype(np.float64) * (ROPE_THETA ** expo)
    cos = np.stack([np.cos(ang), np.cos(ang)], axis=2).reshape(SEQ, QK_ROPE)
    sin = np.stack([-np.sin(ang), np.sin(ang)], axis=2).reshape(SEQ, QK_ROPE)
    cos = np.concatenate([np.ones((CTX_LEN, QK_ROPE)), cos], axis=0)
    sin = np.concatenate([np.zeros((CTX_LEN, QK_ROPE)), sin], axis=0)
    pad = np.zeros((TS, QK_ROPE))
    return (np.concatenate([cos, pad], axis=1).astype(np.float32),
            np.concatenate([sin, pad], axis=1).astype(np.float32))


def _rot_last(a):
    qt = QK_ROPE // 4
    return jnp.concatenate(
        [a[..., qt:2 * qt], a[..., :qt], a[..., 3 * qt:], a[..., 2 * qt:3 * qt]], axis=-1)


def kernel(x, c, ctx, c_ctx, w_ada, b_ada, g_pre_mix, w_in, q_norm, w_uq, kv_norm, w_ukv, w_o_mla,
           w_pool, pool_scale, w_o_pool, w_out, g_post_mix, g_pre_ffn, w_ffn_gate, w_ffn_up,
           w_ffn_down, g_post_ffn):
    assert x.shape == (BATCH, SEQ, D_MODEL) and ctx.shape == (BATCH, CTX_LEN, D_MODEL)
    cvec = jnp.concatenate(
        [c, c_ctx[None, :], jnp.zeros((MOD_ROWS - BATCH - 1, D_MODEL), F32)], axis=0)
    mod_all = _ada_call(cvec, w_ada, b_ada).reshape(DEPTH, MOD_ROWS, N_MOD, D_MODEL)
    cos_t, sin_t = _rope_tables()

    kr_cols = w_in[:, :, OFF_KR:OFF_POOL]
    win_a = jnp.concatenate(
        [w_in[:, :, :OFF_POOL], _rot_last(kr_cols), w_in[:, :, OFF_POOL:OFF_GATE]], axis=-1).astype(BF16)
    wg = w_in[:, :, OFF_GATE:].astype(BF16)
    wq = w_uq.reshape(DEPTH, Q_LORA, N_HEADS, QK_NOPE + QK_ROPE)
    wqn = wq[..., :QK_NOPE].reshape(DEPTH, Q_LORA, N_HEADS * QK_NOPE)
    wq_rope = wq[..., QK_NOPE:]
    wqr = jnp.concatenate([wq_rope, _rot_last(wq_rope)], axis=-1).reshape(
        DEPTH, Q_LORA, N_HEADS * 2 * QK_ROPE)
    wqf, weff = _fold_call(wqn, wqr, w_ukv, w_o_mla)
    wpool, wopool, wout = w_pool.astype(BF16), w_o_pool.astype(BF16), w_out.astype(BF16)
    wgate, wup, wdown = w_ffn_gate.astype(BF16), w_ffn_up.astype(BF16), w_ffn_down.astype(BF16)
    vec = lambda a: a[:, None, :]
    gpre, gpost, qn, kvn = vec(g_pre_mix), vec(g_post_mix), vec(q_norm), vec(kv_norm)
    gpre_f, gpost_f, pscale = vec(g_pre_ffn), vec(g_post_ffn), vec(pool_scale)

    x_args = (ctx, x)
    for l in range(DEPTH):
        last = l == DEPTH - 1
        q, k, ct, u = _inproj_call(l, x_args, mod_all, gpre, win_a, qn, wqf, kvn, cos_t, sin_t)
        if last:
            (o,), o_ctx = _attn_call(q, k, ct, ctx_queries=False), None
        else:
            o, o_ctx = _attn_call(q, k, ct, ctx_queries=True)
        xs1 = _mix_call(l, x_args, mod_all, gpre, wg, o, o_ctx, weff, u, wpool, pscale, wopool,
                        wout, gpost)
        xs = _ffn_call(l, xs1, mod_all, gpre_f, wgate, wup, wdown, gpost_f, with_ctx=not last)
        x_args = (xs,)
    return xs
```

```python
import functools

import jax
import jax.numpy as jnp
import numpy as np
from jax import lax
from jax.experimental import pallas as pl
from jax.experimental.pallas import tpu as pltpu

D_MODEL = 1024
BATCH = 4
SEQ = 4096
DEPTH = 2
GRID_W = 64
CTX_LEN = 256
N_HEADS = 8
QK_NOPE = 128
QK_ROPE = 64
V_DIM = 128
Q_LORA = 256
KV_LORA = 128
POOL_W = 512
POOL_WINDOWS = (2, 4, 8, 16)
N_POOL_GROUPS = 4
POOL_GROUP = POOL_W // N_POOL_GROUPS
D_FF = -(-8 * D_MODEL // (3 * 256)) * 256
N_BRANCH = 2
N_MOD = 6
ROPE_THETA = 10000.0
ROPE_AXIS = QK_ROPE // 2
EPS = 1e-6
ATTN_SCALE = (QK_NOPE + QK_ROPE) ** -0.5
Q_SCALE = ATTN_SCALE * 1.4426950408889634

OFF_KV = Q_LORA
OFF_KR = OFF_KV + KV_LORA
OFF_POOL = OFF_KR + QK_ROPE
OFF_GATE = OFF_POOL + POOL_W

F32 = jnp.float32
BF16 = jnp.bfloat16

TM = 256
TS = CTX_LEN + SEQ
NT = TS // TM
NT_LAT = SEQ // TM
LAT0 = NT - NT_LAT
TQ = TM
TPS = 4
MIX_TILES = 2
NQS = SEQ // (TQ * TPS)
HEAD_W = 2 * KV_LORA
KV_W = QK_NOPE + V_DIM
CT_ROWS = KV_LORA + 16
O_W = N_HEADS * KV_LORA
HALO = 8
MOD_ROWS = 8
ADA_BN = 1536
WIN_ROWS = 256
FOLD_ROWS = 32
VMEM_LIMIT = 52 * 1024 * 1024

assert CTX_LEN == TM and SEQ % TM == 0 and HALO * 2 == max(POOL_WINDOWS)
assert BATCH + 1 <= MOD_ROWS and (N_MOD * D_MODEL) % ADA_BN == 0
assert QK_NOPE == KV_LORA == V_DIM and 2 * QK_ROPE == KV_LORA
assert NQS % 2 == 0


def _layer_spec(arr, l):
    nd = arr.ndim
    return pl.BlockSpec((1,) + arr.shape[1:], lambda *_: (l,) + (0,) * (nd - 1),
                        pipeline_mode=pl.Buffered(1))


def _params(n_grid):
    return pltpu.CompilerParams(dimension_semantics=("arbitrary",) * n_grid,
                                vmem_limit_bytes=VMEM_LIMIT)


def _rms(x, g):
    return x * lax.rsqrt(jnp.mean(x * x, axis=-1, keepdims=True) + EPS) * g


def _dot(a, b):
    return jnp.dot(a, b, preferred_element_type=F32)


def _dot_t(a, b):
    return lax.dot_general(a, b, (((1,), (1,)), ((), ())), preferred_element_type=F32)


def _ada_kernel(c_ref, w_ref, b_ref, o_ref):
    cv = c_ref[...]
    s = cv * jax.nn.sigmoid(cv)
    o_ref[0] = _dot(s.astype(BF16), w_ref[0].astype(BF16)) + b_ref[0]


def _ada_call(cvec, w_ada, b_ada):
    n_out = N_MOD * D_MODEL
    return pl.pallas_call(
        _ada_kernel,
        grid=(DEPTH, n_out // ADA_BN),
        in_specs=[
            pl.BlockSpec((MOD_ROWS, D_MODEL), lambda l, n: (0, 0)),
            pl.BlockSpec((1, D_MODEL, ADA_BN), lambda l, n: (l, 0, n)),
            pl.BlockSpec((1, 1, ADA_BN), lambda l, n: (l, 0, n)),
        ],
        out_specs=pl.BlockSpec((1, MOD_ROWS, ADA_BN), lambda l, n: (l, 0, n)),
        out_shape=jax.ShapeDtypeStruct((DEPTH, MOD_ROWS, n_out), F32),
        compiler_params=_params(2),
        name="ada_mod",
    )(cvec, w_ada, b_ada.reshape(DEPTH, 1, n_out))


def _mod_spec(l, off):
    return pl.BlockSpec((1, 1, N_MOD, D_MODEL),
                        lambda b, j: (l, jnp.where(j + off == 0, BATCH, b), 0, 0))


def _fold_kernel(wqn_ref, wqr_ref, wukv_ref, womla_ref, wqf_ref, weff_ref):
    for hd in range(N_HEADS):
        n0 = hd * QK_NOPE
        c0 = hd * KV_W
        q0 = hd * HEAD_W
        wuk = wukv_ref[0, :, c0:c0 + QK_NOPE].astype(BF16)
        wuv = wukv_ref[0, :, c0 + QK_NOPE:c0 + KV_W].astype(BF16)
        wqf_ref[0, :, q0:q0 + KV_LORA] = _dot_t(
            wqn_ref[0, :, n0:n0 + QK_NOPE].astype(BF16), wuk).astype(BF16)
        wqf_ref[0, :, q0 + KV_LORA:q0 + HEAD_W] = wqr_ref[0, :, n0:n0 + 2 * QK_ROPE].astype(BF16)
        weff_ref[0, hd * KV_LORA:(hd + 1) * KV_LORA, :] = _dot(
            wuv, womla_ref[0, hd * V_DIM:(hd + 1) * V_DIM, :].astype(BF16)).astype(BF16)


def _fold_call(wqn, wqr, w_ukv, w_o_mla):
    layer = lambda a: pl.BlockSpec((1,) + a.shape[1:], lambda l: (l, 0, 0))
    return pl.pallas_call(
        _fold_kernel,
        grid=(DEPTH,),
        in_specs=[layer(wqn), layer(wqr), layer(w_ukv), layer(w_o_mla)],
        out_specs=[pl.BlockSpec((1, Q_LORA, N_HEADS * HEAD_W), lambda l: (l, 0, 0)),
                   pl.BlockSpec((1, O_W, D_MODEL), lambda l: (l, 0, 0))],
        out_shape=[jax.ShapeDtypeStruct((DEPTH, Q_LORA, N_HEADS * HEAD_W), BF16),
                   jax.ShapeDtypeStruct((DEPTH, O_W, D_MODEL), BF16)],
        compiler_params=_params(1),
        name="fold",
    )(wqn, wqr, w_ukv, w_o_mla)


def _win_kernel(w_ref, wa_ref, wg_ref):
    w = w_ref[0]
    wa_ref[0, :, :OFF_POOL] = w[:, :OFF_POOL].astype(BF16)
    wa_ref[0, :, OFF_POOL:OFF_POOL + QK_ROPE] = _rot_last(w[:, OFF_KR:OFF_POOL]).astype(BF16)
    wa_ref[0, :, OFF_POOL + QK_ROPE:] = w[:, OFF_POOL:OFF_GATE].astype(BF16)
    wg_ref[0] = w[:, OFF_GATE:].astype(BF16)


def _win_call(w_in):
    d_in = w_in.shape[-1]
    block = lambda width: pl.BlockSpec((1, WIN_ROWS, width), lambda l, r: (l, r, 0))
    return pl.pallas_call(
        _win_kernel,
        grid=(DEPTH, D_MODEL // WIN_ROWS),
        in_specs=[block(d_in)],
        out_specs=[block(OFF_GATE + QK_ROPE), block(d_in - OFF_GATE)],
        out_shape=[jax.ShapeDtypeStruct((DEPTH, D_MODEL, OFF_GATE + QK_ROPE), BF16),
                   jax.ShapeDtypeStruct((DEPTH, D_MODEL, d_in - OFF_GATE), BF16)],
        compiler_params=_params(2),
        name="win_layout",
    )(w_in)


def _row_specs(separate, off):
    if separate:
        return [pl.BlockSpec((1, TM, D_MODEL), lambda b, j: (b, 0, 0)),
                pl.BlockSpec((1, TM, D_MODEL), lambda b, j: (b, jnp.maximum(j + off - LAT0, 0), 0))]
    return [pl.BlockSpec((1, TM, D_MODEL), lambda b, j: (b, j + off, 0))]


def _load_rows(x_refs, j):
    if len(x_refs) == 2:
        return jnp.where(j == 0, x_refs[0][0], x_refs[1][0])
    return x_refs[0][0]


def _rope(t, cos, sin):
    return t * cos + pltpu.roll(t, QK_ROPE, axis=1) * sin


def _inproj_kernel(*refs, n_x):
    x_refs = refs[:n_x]
    (mod_ref, gpre_ref, win_ref, qn_ref, wqf_ref, kvn_ref, cos_ref, sin_ref,
     q_ref, k_ref, ct_ref, u_ref) = refs[n_x:]
    x = _load_rows(x_refs, pl.program_id(1))
    m = mod_ref[0, 0]
    h = _rms(x, gpre_ref[0]) * (1.0 + m[0:1]) + m[1:2]
    z = _dot(h.astype(BF16), win_ref[0])
    u_ref[0] = z[:, OFF_POOL + QK_ROPE:]
    cos = cos_ref[...]
    sin = sin_ref[...]

    qn = _rms(z[:, :OFF_KV], qn_ref[0])
    qa = _dot(qn.astype(BF16), wqf_ref[0])
    for hd in range(N_HEADS):
        c0 = hd * HEAD_W
        q_ref[0, hd, :, :KV_LORA] = (qa[:, c0:c0 + KV_LORA] * Q_SCALE).astype(BF16)
        q_ref[0, hd, :, KV_LORA:] = (
            _rope(qa[:, c0 + KV_LORA:c0 + HEAD_W], cos, sin) * Q_SCALE).astype(BF16)

    cn = _rms(z[:, OFF_KV:OFF_KR], kvn_ref[0])
    k_ref[0, :, :KV_LORA] = cn.astype(BF16)
    k_ref[0, :, KV_LORA:] = _rope(z[:, OFF_KR:OFF_KR + 2 * QK_ROPE], cos, sin).astype(BF16)
    ct_ref[0, :KV_LORA] = cn.T.astype(BF16)
    ct_ref[0, KV_LORA:] = (
        lax.broadcasted_iota(jnp.int32, (CT_ROWS - KV_LORA, TM), 0) == 0).astype(BF16)


def _inproj_call(l, x_args, mod_all, gpre, win_a, qnorm, wqf, kvnorm, cos_t, sin_t):
    tile = lambda b, j: (b, j, 0)
    return pl.pallas_call(
        functools.partial(_inproj_kernel, n_x=len(x_args)),
        grid=(BATCH, NT),
        in_specs=[
            *_row_specs(len(x_args) == 2, 0),
            _mod_spec(l, 0),
            _layer_spec(gpre, l),
            _layer_spec(win_a, l),
            _layer_spec(qnorm, l),
            _layer_spec(wqf, l),
            _layer_spec(kvnorm, l),
            pl.BlockSpec((TM, 2 * QK_ROPE), lambda b, j: (j, 0)),
            pl.BlockSpec((TM, 2 * QK_ROPE), lambda b, j: (j, 0)),
        ],
        out_specs=[
            pl.BlockSpec((1, N_HEADS, TM, HEAD_W), lambda b, j: (b, 0, j, 0)),
            pl.BlockSpec((1, TM, HEAD_W), tile),
            pl.BlockSpec((1, CT_ROWS, TM), lambda b, j: (b, 0, j)),
            pl.BlockSpec((1, TM, POOL_W), tile),
        ],
        out_shape=[
            jax.ShapeDtypeStruct((BATCH, N_HEADS, TS, HEAD_W), BF16),
            jax.ShapeDtypeStruct((BATCH, TS, HEAD_W), BF16),
            jax.ShapeDtypeStruct((BATCH, CT_ROWS, TS), BF16),
            jax.ShapeDtypeStruct((BATCH, TS, POOL_W), F32),
        ],
        compiler_params=_params(2),
        name="inproj",
    )(*x_args, mod_all, gpre, win_a, qnorm, wqf, kvnorm, cos_t, sin_t)


def _col_fold(x, pair, reduce):
    acc = x[:FOLD_ROWS]
    for r in range(FOLD_ROWS, x.shape[0], FOLD_ROWS):
        acc = pair(acc, x[r:r + FOLD_ROWS])
    return reduce(acc, axis=0, keepdims=True)


def _col_max(s):
    return _col_fold(s, jnp.maximum, jnp.max)


def _softmax_pc_t(s, m, ct):
    p = jnp.exp2(s - m)
    ot = _dot(ct, p.astype(BF16))
    return (ot[:KV_LORA] / ot[KV_LORA:KV_LORA + 1]).T


def _attn_kernel(*refs, ctx_queries):
    qn_refs, q0_refs, refs = refs[:TPS], refs[TPS:2 * TPS], refs[2 * TPS:]
    if ctx_queries:
        qc_ref, k_ref, ct_ref, o_ref, octx_ref, *scratch = refs
    else:
        k_ref, ct_ref, o_ref, *scratch = refs
    sa, sb = scratch[:TPS], scratch[TPS:2 * TPS]
    ma, mb = scratch[2 * TPS:3 * TPS], scratch[3 * TPS:]
    b, h, i = pl.program_id(0), pl.program_id(1), pl.program_id(2)

    def scores_stage(q_ref, s_ref, m_ref):
        s = _dot_t(k_ref[0], q_ref[0, 0])
        s_ref[...] = s
        m_ref[...] = _col_max(s)

    def output_stage(t, s_ref, m_ref):
        o_ref[0, t * TQ:(t + 1) * TQ, :] = _softmax_pc_t(
            s_ref[...], m_ref[...], ct_ref[0]).astype(BF16)

    def step(s_in, m_in, s_out, m_out):
        for t in range(TPS):
            scores_stage(qn_refs[t], s_out[t], m_out[t])
            output_stage(t, s_in[t], m_in[t])

    @pl.when(jnp.logical_and(jnp.logical_and(b == 0, h == 0), i == 0))
    def _():
        for t in range(TPS):
            scores_stage(q0_refs[t], sa[t], ma[t])

    if ctx_queries:
        @pl.when(i == 0)
        def _():
            sc = _dot_t(k_ref[0, :CTX_LEN], qc_ref[0, 0])
            octx_ref[0] = _softmax_pc_t(sc, _col_max(sc), ct_ref[0, :, :CTX_LEN]).astype(BF16)

    @pl.when(i % 2 == 0)
    def _():
        step(sa, ma, sb, mb)

    @pl.when(i % 2 == 1)
    def _():
        step(sb, mb, sa, ma)


def _next_step(b, h, i):
    f = jnp.minimum((b * N_HEADS + h) * NQS + i + 1, BATCH * N_HEADS * NQS - 1)
    return f // (N_HEADS * NQS), (f // NQS) % N_HEADS, f % NQS


def _attn_call(q, k, ct, ctx_queries):
    def next_q(t):
        def index(b, h, i):
            nb, nh, ng = _next_step(b, h, i)
            return (nb, nh, ng * TPS + t + LAT0, 0)
        return index

    q_tile = (1, 1, TQ, HEAD_W)
    in_specs = [pl.BlockSpec(q_tile, next_q(t)) for t in range(TPS)]
    in_specs += [pl.BlockSpec(q_tile, functools.partial(lambda t, b, h, i: (0, 0, LAT0 + t, 0), t))
                 for t in range(TPS)]
    args = [q] * (2 * TPS)
    out_specs = [pl.BlockSpec((1, TPS * TQ, KV_LORA), lambda b, h, i: (b, i, h))]
    out_shape = [jax.ShapeDtypeStruct((BATCH, SEQ, O_W), BF16)]
    if ctx_queries:
        in_specs.append(pl.BlockSpec(q_tile, lambda b, h, i: (b, h, 0, 0)))
        args.append(q)
        out_specs.append(pl.BlockSpec((1, CTX_LEN, KV_LORA), lambda b, h, i: (b, 0, h)))
        out_shape.append(jax.ShapeDtypeStruct((BATCH, CTX_LEN, O_W), BF16))
    in_specs += [
        pl.BlockSpec((1, TS, HEAD_W), lambda b, h, i: (_next_step(b, h, i)[0], 0, 0)),
        pl.BlockSpec((1, CT_ROWS, TS), lambda b, h, i: (b, 0, 0)),
    ]
    return pl.pallas_call(
        functools.partial(_attn_kernel, ctx_queries=ctx_queries),
        grid=(BATCH, N_HEADS, NQS),
        in_specs=in_specs,
        out_specs=out_specs,
        out_shape=out_shape,
        scratch_shapes=([pltpu.VMEM((TS, TQ), F32)] * (2 * TPS)
                        + [pltpu.VMEM((1, TQ), F32)] * (2 * TPS)),
        compiler_params=_params(3),
        name="attention",
    )(*args, k, ct)


def _sigmoid(z):
    return 0.5 * (jnp.tanh(0.5 * z) + 1.0)


def _mix_kernel(*refs, n_x, n_t, tile_off, with_ctx):
    n_tile = n_x + 5 + int(with_ctx)
    (gpre_ref, wg_ref, weff_ref, wpool_ref, pscale_ref, wopool_ref, wout_ref, gpost_ref,
     out_ref, *ext_refs) = refs[MIX_TILES * n_tile:]
    for t in range(MIX_TILES):
        x_refs = refs[t * n_tile:t * n_tile + n_x]
        mod_ref, o_ref, *rest = refs[t * n_tile + n_x:(t + 1) * n_tile]
        if with_ctx:
            octx_ref, *rest = rest
        uc_ref, up_ref, un_ref = rest
        ext_ref = ext_refs[t]
        j = (pl.program_id(0) * MIX_TILES + t) % n_t + tile_off
        x = _load_rows(x_refs, j)
        m = mod_ref[0, 0]
        h = (_rms(x, gpre_ref[0]) * (1.0 + m[0:1]) + m[1:2]).astype(BF16)
        gates = _sigmoid(_dot(h, wg_ref[0]))
        o = jnp.where(j == 0, octx_ref[0], o_ref[0]) if with_ctx else o_ref[0]
        att = _dot(o, weff_ref[0])

        is_ctx = j == 0
        first = jnp.logical_or(is_ctx, j == LAT0)
        last = jnp.logical_or(is_ctx, j == NT - 1)
        u = uc_ref[0]
        ext_ref[0:HALO] = jnp.where(first, 0.0, up_ref[0])
        ext_ref[HALO:HALO + TM] = u
        ext_ref[HALO + TM:] = jnp.where(last, 0.0, un_ref[0])
        seq_len = jnp.where(is_ctx, CTX_LEN, SEQ)
        pos = lax.broadcasted_iota(jnp.int32, (TM, 1), 0) + jnp.where(is_ctx, 0, j - LAT0) * TM
        ys = []
        for g, w in enumerate(POOL_WINDOWS):
            lanes = slice(g * POOL_GROUP, (g + 1) * POOL_GROUP)
            acc = ext_ref[HALO - w // 2:HALO - w // 2 + TM, lanes]
            for s in range(1 - w // 2, w - w // 2):
                acc = acc + ext_ref[HALO + s:HALO + s + TM, lanes]
            cnt = jnp.minimum(pos + (w - w // 2), seq_len) - jnp.maximum(pos - w // 2, 0)
            d = acc / cnt.astype(F32) - u[:, lanes]
            ys.append(_dot(d.astype(BF16), wpool_ref[0, g]))
        yp = jnp.concatenate(ys, axis=1) * pscale_ref[0]
        o_pool = _dot(yp.astype(BF16), wopool_ref[0])

        merged = gates[:, :D_MODEL] * att + gates[:, D_MODEL:] * o_pool
        y = _dot(merged.astype(BF16), wout_ref[0])
        out_ref[t * TM:(t + 1) * TM, :] = x + m[2:3] * _rms(y, gpost_ref[0])


def _mix_call(l, x_args, mod_all, gpre, wg, o, o_ctx, weff, u, wpool, pscale, wopool, wout, gpost):
    with_ctx = o_ctx is not None
    n_t = NT if with_ctx else NT_LAT
    off = 0 if with_ctx else LAT0
    hb = TM // HALO
    assert (BATCH * n_t) % MIX_TILES == 0

    def tile_specs(t):
        def at(fn):
            def index(p):
                f = p * MIX_TILES + t
                return fn(f // n_t, f % n_t + off)
            return index

        lat = lambda b, j: (b, jnp.maximum(j - LAT0, 0), 0)
        specs = ([pl.BlockSpec((1, TM, D_MODEL), at(lambda b, j: (b, 0, 0))),
                  pl.BlockSpec((1, TM, D_MODEL), at(lat))] if len(x_args) == 2
                 else [pl.BlockSpec((1, TM, D_MODEL), at(lambda b, j: (b, j, 0)))])
        specs.append(pl.BlockSpec((1, 1, N_MOD, D_MODEL),
                                  at(lambda b, j: (l, jnp.where(j == 0, BATCH, b), 0, 0))))
        specs.append(pl.BlockSpec((1, TM, O_W), at(lat)))
        if with_ctx:
            specs.append(pl.BlockSpec((1, CTX_LEN, O_W), at(lambda b, j: (b, 0, 0))))
        specs += [
            pl.BlockSpec((1, TM, POOL_W), at(lambda b, j: (b, j, 0))),
            pl.BlockSpec((1, HALO, POOL_W), at(lambda b, j: (b, jnp.maximum(j * hb - 1, 0), 0))),
            pl.BlockSpec((1, HALO, POOL_W),
                         at(lambda b, j: (b, jnp.minimum((j + 1) * hb, TS // HALO - 1), 0))),
        ]
        return specs

    tile_args = [*x_args, mod_all, o, *([o_ctx] if with_ctx else []), u, u, u]
    weights = [gpre, wg, weff, wpool, pscale, wopool, wout, gpost]
    out = pl.pallas_call(
        functools.partial(_mix_kernel, n_x=len(x_args), n_t=n_t, tile_off=off, with_ctx=with_ctx),
        grid=(BATCH * n_t // MIX_TILES,),
        in_specs=([s for t in range(MIX_TILES) for s in tile_specs(t)]
                  + [_layer_spec(w, l) for w in weights]),
        out_specs=pl.BlockSpec((MIX_TILES * TM, D_MODEL), lambda p: (p, 0)),
        out_shape=jax.ShapeDtypeStruct((BATCH * n_t * TM, D_MODEL), F32),
        scratch_shapes=[pltpu.VMEM((TM + 2 * HALO, POOL_W), F32)] * MIX_TILES,
        compiler_params=_params(1),
        name="mix",
    )(*(tile_args * MIX_TILES), *weights)
    return out.reshape(BATCH, n_t * TM, D_MODEL)


def _ffn_kernel(x_ref, mod_ref, gpre_ref, wgate_ref, wup_ref, wdown_ref, gpost_ref, out_ref):
    x = x_ref[0]
    m = mod_ref[0, 0]
    h = (_rms(x, gpre_ref[0]) * (1.0 + m[3:4]) + m[4:5]).astype(BF16)
    a = _dot(h, wgate_ref[0])
    t = (a * _sigmoid(a)) * _dot(h, wup_ref[0])
    y = _dot(t.astype(BF16), wdown_ref[0])
    out_ref[0] = x + m[5:6] * _rms(y, gpost_ref[0])


def _ffn_call(l, xs, mod_all, gpre, wgate, wup, wdown, gpost, with_ctx):
    n_t = NT if with_ctx else NT_LAT
    off = 0 if with_ctx else LAT0
    tile = lambda b, j: (b, j, 0)
    if not with_ctx:
        rows = 2 * TM
        return pl.pallas_call(
            _ffn_kernel,
            grid=(BATCH, SEQ // rows),
            in_specs=[
                pl.BlockSpec((1, rows, D_MODEL), tile),
                _mod_spec(l, off),
                _layer_spec(gpre, l),
                _layer_spec(wgate, l),
                _layer_spec(wup, l),
                _layer_spec(wdown, l),
                _layer_spec(gpost, l),
            ],
            out_specs=pl.BlockSpec((1, rows, D_MODEL), tile),
            out_shape=jax.ShapeDtypeStruct((BATCH, SEQ, D_MODEL), F32),
            compiler_params=_params(2),
            name="ffn",
        )(xs, mod_all, gpre, wgate, wup, wdown, gpost)
    return pl.pallas_call(
        _ffn_kernel,
        grid=(BATCH, n_t),
        in_specs=[
            pl.BlockSpec((1, TM, D_MODEL), tile),
            _mod_spec(l, off),
            _layer_spec(gpre, l),
            _layer_spec(wgate, l),
            _layer_spec(wup, l),
            _layer_spec(wdown, l),
            _layer_spec(gpost, l),
        ],
        out_specs=pl.BlockSpec((1, TM, D_MODEL), tile),
        out_shape=jax.ShapeDtypeStruct((BATCH, n_t * TM, D_MODEL), F32),
        compiler_params=_params(2),
        name="ffn",
    )(xs, mod_all, gpre, wgate, wup, wdown, gpost)


def _rope_tables():
    rows = SEQ // GRID_W
    pos = np.stack([np.repeat(np.arange(rows), GRID_W), np.tile(np.arange(GRID_W), rows)], axis=-1)
    expo = (-np.arange(0, ROPE_AXIS, 2, dtype=np.float32) / np.float32(ROPE_AXIS)).astype(np.float64)
    ang = pos[:, :, None].astype(np.float64) * (ROPE_THETA ** expo)
    cos = np.stack([np.cos(ang), np.cos(ang)], axis=2).reshape(SEQ, QK_ROPE)
    sin = np.stack([-np.sin(ang), np.sin(ang)], axis=2).reshape(SEQ, QK_ROPE)
    cos = np.concatenate([np.ones((CTX_LEN, QK_ROPE)), cos], axis=0)
    sin = np.concatenate([np.zeros((CTX_LEN, QK_ROPE)), sin], axis=0)
    pad = np.zeros((TS, QK_ROPE))
    return (np.concatenate([cos, pad], axis=1).astype(np.float32),
            np.concatenate([sin, pad], axis=1).astype(np.float32))


def _rot_last(a):
    qt = QK_ROPE // 4
    return jnp.concatenate(
        [a[..., qt:2 * qt], a[..., :qt], a[..., 3 * qt:], a[..., 2 * qt:3 * qt]], axis=-1)


def kernel(x, c, ctx, c_ctx, w_ada, b_ada, g_pre_mix, w_in, q_norm, w_uq, kv_norm, w_ukv, w_o_mla,
           w_pool, pool_scale, w_o_pool, w_out, g_post_mix, g_pre_ffn, w_ffn_gate, w_ffn_up,
           w_ffn_down, g_post_ffn):
    assert x.shape == (BATCH, SEQ, D_MODEL) and ctx.shape == (BATCH, CTX_LEN, D_MODEL)
    cvec = jnp.concatenate(
        [c, c_ctx[None, :], jnp.zeros((MOD_ROWS - BATCH - 1, D_MODEL), F32)], axis=0)
    mod_all = _ada_call(cvec, w_ada, b_ada).reshape(DEPTH, MOD_ROWS, N_MOD, D_MODEL)
    cos_t, sin_t = _rope_tables()

    win_a, wg = _win_call(w_in)
    wq = w_uq.reshape(DEPTH, Q_LORA, N_HEADS, QK_NOPE + QK_ROPE)
    wqn = wq[..., :QK_NOPE].reshape(DEPTH, Q_LORA, N_HEADS * QK_NOPE)
    wq_rope = wq[..., QK_NOPE:]
    wqr = jnp.concatenate([wq_rope, _rot_last(wq_rope)], axis=-1).reshape(
        DEPTH, Q_LORA, N_HEADS * 2 * QK_ROPE)
    wqf, weff = _fold_call(wqn, wqr, w_ukv, w_o_mla)
    wpool, wopool, wout = w_pool.astype(BF16), w_o_pool.astype(BF16), w_out.astype(BF16)
    wgate, wup, wdown = w_ffn_gate.astype(BF16), w_ffn_up.astype(BF16), w_ffn_down.astype(BF16)
    vec = lambda a: a[:, None, :]
    gpre, gpost, qn, kvn = vec(g_pre_mix), vec(g_post_mix), vec(q_norm), vec(kv_norm)
    gpre_f, gpost_f, pscale = vec(g_pre_ffn), vec(g_post_ffn), vec(pool_scale)

    x_args = (ctx, x)
    for l in range(DEPTH):
        last = l == DEPTH - 1
        q, k, ct, u = _inproj_call(l, x_args, mod_all, gpre, win_a, qn, wqf, kvn, cos_t, sin_t)
        if last:
            (o,), o_ctx = _attn_call(q, k, ct, ctx_queries=False), None
        else:
            o, o_ctx = _attn_call(q, k, ct, ctx_queries=True)
        xs1 = _mix_call(l, x_args, mod_all, gpre, wg, o, o_ctx, weff, u, wpool, pscale, wopool,
                        wout, gpost)
        xs = _ffn_call(l, xs1, mod_all, gpre_f, wgate, wup, wdown, gpost_f, with_ctx=not last)
        x_args = (xs,)
    return xs
```

```python
import functools

import jax
import jax.numpy as jnp
import numpy as np
from jax import lax
from jax.experimental import pallas as pl
from jax.experimental.pallas import tpu as pltpu

D_MODEL = 1024
BATCH = 4
SEQ = 4096
DEPTH = 2
GRID_W = 64
CTX_LEN = 256
N_HEADS = 8
QK_NOPE = 128
QK_ROPE = 64
V_DIM = 128
Q_LORA = 256
KV_LORA = 128
POOL_W = 512
POOL_WINDOWS = (2, 4, 8, 16)
N_POOL_GROUPS = 4
POOL_GROUP = POOL_W // N_POOL_GROUPS
D_FF = -(-8 * D_MODEL // (3 * 256)) * 256
N_BRANCH = 2
N_MOD = 6
ROPE_THETA = 10000.0
ROPE_AXIS = QK_ROPE // 2
EPS = 1e-6
ATTN_SCALE = (QK_NOPE + QK_ROPE) ** -0.5
Q_SCALE = ATTN_SCALE * 1.4426950408889634

OFF_KV = Q_LORA
OFF_KR = OFF_KV + KV_LORA
OFF_POOL = OFF_KR + QK_ROPE
OFF_GATE = OFF_POOL + POOL_W

F32 = jnp.float32
BF16 = jnp.bfloat16

TM = 256
TS = CTX_LEN + SEQ
NT = TS // TM
NT_LAT = SEQ // TM
LAT0 = NT - NT_LAT
TQ = TM
TPS = 4
MIX_TILES = 2
FFN_TILES = 4
NQS = SEQ // (TQ * TPS)
HEAD_W = 2 * KV_LORA
KV_W = QK_NOPE + V_DIM
CT_ROWS = KV_LORA + 16
O_W = N_HEADS * KV_LORA
HALO = 8
MOD_ROWS = 8
ADA_BN = 1536
WIN_ROWS = 256
FOLD_ROWS = 32
VMEM_LIMIT = 52 * 1024 * 1024

assert CTX_LEN == TM and SEQ % TM == 0 and HALO * 2 == max(POOL_WINDOWS)
assert BATCH + 1 <= MOD_ROWS and (N_MOD * D_MODEL) % ADA_BN == 0
assert QK_NOPE == KV_LORA == V_DIM and 2 * QK_ROPE == KV_LORA
assert NQS % 2 == 0


def _layer_spec(arr, l):
    nd = arr.ndim
    return pl.BlockSpec((1,) + arr.shape[1:], lambda *_: (l,) + (0,) * (nd - 1),
                        pipeline_mode=pl.Buffered(1))


def _params(n_grid):
    return pltpu.CompilerParams(dimension_semantics=("arbitrary",) * n_grid,
                                vmem_limit_bytes=VMEM_LIMIT)


def _rms(x, g):
    return x * lax.rsqrt(jnp.mean(x * x, axis=-1, keepdims=True) + EPS) * g


def _dot(a, b):
    return jnp.dot(a, b, preferred_element_type=F32)


def _dot_t(a, b):
    return lax.dot_general(a, b, (((1,), (1,)), ((), ())), preferred_element_type=F32)


def _ada_kernel(c_ref, w_ref, b_ref, o_ref):
    cv = c_ref[...]
    s = cv * jax.nn.sigmoid(cv)
    o_ref[0] = _dot(s.astype(BF16), w_ref[0].astype(BF16)) + b_ref[0]


def _ada_call(cvec, w_ada, b_ada):
    n_out = N_MOD * D_MODEL
    return pl.pallas_call(
        _ada_kernel,
        grid=(DEPTH, n_out // ADA_BN),
        in_specs=[
            pl.BlockSpec((MOD_ROWS, D_MODEL), lambda l, n: (0, 0)),
            pl.BlockSpec((1, D_MODEL, ADA_BN), lambda l, n: (l, 0, n)),
            pl.BlockSpec((1, 1, ADA_BN), lambda l, n: (l, 0, n)),
        ],
        out_specs=pl.BlockSpec((1, MOD_ROWS, ADA_BN), lambda l, n: (l, 0, n)),
        out_shape=jax.ShapeDtypeStruct((DEPTH, MOD_ROWS, n_out), F32),
        compiler_params=_params(2),
        name="ada_mod",
    )(cvec, w_ada, b_ada.reshape(DEPTH, 1, n_out))


def _mod_spec(l, off):
    return pl.BlockSpec((1, 1, N_MOD, D_MODEL),
                        lambda b, j: (l, jnp.where(j + off == 0, BATCH, b), 0, 0))


def _fold_kernel(wqn_ref, wqr_ref, wukv_ref, womla_ref, wqf_ref, weff_ref):
    for hd in range(N_HEADS):
        n0 = hd * QK_NOPE
        c0 = hd * KV_W
        q0 = hd * HEAD_W
        wuk = wukv_ref[0, :, c0:c0 + QK_NOPE].astype(BF16)
        wuv = wukv_ref[0, :, c0 + QK_NOPE:c0 + KV_W].astype(BF16)
        wqf_ref[0, :, q0:q0 + KV_LORA] = _dot_t(
            wqn_ref[0, :, n0:n0 + QK_NOPE].astype(BF16), wuk).astype(BF16)
        wqf_ref[0, :, q0 + KV_LORA:q0 + HEAD_W] = wqr_ref[0, :, n0:n0 + 2 * QK_ROPE].astype(BF16)
        weff_ref[0, hd * KV_LORA:(hd + 1) * KV_LORA, :] = _dot(
            wuv, womla_ref[0, hd * V_DIM:(hd + 1) * V_DIM, :].astype(BF16)).astype(BF16)


def _fold_call(wqn, wqr, w_ukv, w_o_mla):
    layer = lambda a: pl.BlockSpec((1,) + a.shape[1:], lambda l: (l, 0, 0))
    return pl.pallas_call(
        _fold_kernel,
        grid=(DEPTH,),
        in_specs=[layer(wqn), layer(wqr), layer(w_ukv), layer(w_o_mla)],
        out_specs=[pl.BlockSpec((1, Q_LORA, N_HEADS * HEAD_W), lambda l: (l, 0, 0)),
                   pl.BlockSpec((1, O_W, D_MODEL), lambda l: (l, 0, 0))],
        out_shape=[jax.ShapeDtypeStruct((DEPTH, Q_LORA, N_HEADS * HEAD_W), BF16),
                   jax.ShapeDtypeStruct((DEPTH, O_W, D_MODEL), BF16)],
        compiler_params=_params(1),
        name="fold",
    )(wqn, wqr, w_ukv, w_o_mla)


def _win_kernel(w_ref, wa_ref, wg_ref):
    w = w_ref[0]
    wa_ref[0, :, :OFF_POOL] = w[:, :OFF_POOL].astype(BF16)
    wa_ref[0, :, OFF_POOL:OFF_POOL + QK_ROPE] = _rot_last(w[:, OFF_KR:OFF_POOL]).astype(BF16)
    wa_ref[0, :, OFF_POOL + QK_ROPE:] = w[:, OFF_POOL:OFF_GATE].astype(BF16)
    wg_ref[0] = (0.5 * w[:, OFF_GATE:]).astype(BF16)


def _win_call(w_in):
    d_in = w_in.shape[-1]
    block = lambda width: pl.BlockSpec((1, WIN_ROWS, width), lambda l, r: (l, r, 0))
    return pl.pallas_call(
        _win_kernel,
        grid=(DEPTH, D_MODEL // WIN_ROWS),
        in_specs=[block(d_in)],
        out_specs=[block(OFF_GATE + QK_ROPE), block(d_in - OFF_GATE)],
        out_shape=[jax.ShapeDtypeStruct((DEPTH, D_MODEL, OFF_GATE + QK_ROPE), BF16),
                   jax.ShapeDtypeStruct((DEPTH, D_MODEL, d_in - OFF_GATE), BF16)],
        compiler_params=_params(2),
        name="win_layout",
    )(w_in)


def _row_specs(separate, off):
    if separate:
        return [pl.BlockSpec((1, TM, D_MODEL), lambda b, j: (b, 0, 0)),
                pl.BlockSpec((1, TM, D_MODEL), lambda b, j: (b, jnp.maximum(j + off - LAT0, 0), 0))]
    return [pl.BlockSpec((1, TM, D_MODEL), lambda b, j: (b, j + off, 0))]


def _load_rows(x_refs, j):
    if len(x_refs) == 2:
        return jnp.where(j == 0, x_refs[0][0], x_refs[1][0])
    return x_refs[0][0]


def _rope(t, cos, sin):
    return t * cos + pltpu.roll(t, QK_ROPE, axis=1) * sin


def _inproj_kernel(*refs, n_x):
    x_refs = refs[:n_x]
    (mod_ref, gpre_ref, win_ref, qn_ref, wqf_ref, kvn_ref, cos_ref, sin_ref,
     q_ref, k_ref, ct_ref, u_ref) = refs[n_x:]
    x = _load_rows(x_refs, pl.program_id(1))
    m = mod_ref[0, 0]
    h = _rms(x, gpre_ref[0]) * (1.0 + m[0:1]) + m[1:2]
    z = _dot(h.astype(BF16), win_ref[0])
    u_ref[0] = z[:, OFF_POOL + QK_ROPE:]
    cos = cos_ref[...]
    sin = sin_ref[...]

    qn = _rms(z[:, :OFF_KV], qn_ref[0])
    qa = _dot(qn.astype(BF16), wqf_ref[0])
    for hd in range(N_HEADS):
        c0 = hd * HEAD_W
        q_ref[0, hd, :, :KV_LORA] = (qa[:, c0:c0 + KV_LORA] * Q_SCALE).astype(BF16)
        q_ref[0, hd, :, KV_LORA:] = (
            _rope(qa[:, c0 + KV_LORA:c0 + HEAD_W], cos, sin) * Q_SCALE).astype(BF16)

    cn = _rms(z[:, OFF_KV:OFF_KR], kvn_ref[0])
    k_ref[0, :, :KV_LORA] = cn.astype(BF16)
    k_ref[0, :, KV_LORA:] = _rope(z[:, OFF_KR:OFF_KR + 2 * QK_ROPE], cos, sin).astype(BF16)
    ct_ref[0, :KV_LORA] = cn.T.astype(BF16)
    ct_ref[0, KV_LORA:] = (
        lax.broadcasted_iota(jnp.int32, (CT_ROWS - KV_LORA, TM), 0) == 0).astype(BF16)


def _inproj_call(l, x_args, mod_all, gpre, win_a, qnorm, wqf, kvnorm, cos_t, sin_t):
    tile = lambda b, j: (b, j, 0)
    return pl.pallas_call(
        functools.partial(_inproj_kernel, n_x=len(x_args)),
        grid=(BATCH, NT),
        in_specs=[
            *_row_specs(len(x_args) == 2, 0),
            _mod_spec(l, 0),
            _layer_spec(gpre, l),
            _layer_spec(win_a, l),
            _layer_spec(qnorm, l),
            _layer_spec(wqf, l),
            _layer_spec(kvnorm, l),
            pl.BlockSpec((TM, 2 * QK_ROPE), lambda b, j: (j, 0)),
            pl.BlockSpec((TM, 2 * QK_ROPE), lambda b, j: (j, 0)),
        ],
        out_specs=[
            pl.BlockSpec((1, N_HEADS, TM, HEAD_W), lambda b, j: (b, 0, j, 0)),
            pl.BlockSpec((1, TM, HEAD_W), tile),
            pl.BlockSpec((1, CT_ROWS, TM), lambda b, j: (b, 0, j)),
            pl.BlockSpec((1, TM, POOL_W), tile),
        ],
        out_shape=[
            jax.ShapeDtypeStruct((BATCH, N_HEADS, TS, HEAD_W), BF16),
            jax.ShapeDtypeStruct((BATCH, TS, HEAD_W), BF16),
            jax.ShapeDtypeStruct((BATCH, CT_ROWS, TS), BF16),
            jax.ShapeDtypeStruct((BATCH, TS, POOL_W), F32),
        ],
        compiler_params=_params(2),
        name="inproj",
    )(*x_args, mod_all, gpre, win_a, qnorm, wqf, kvnorm, cos_t, sin_t)


def _col_fold(x, pair, reduce):
    acc = x[:FOLD_ROWS]
    for r in range(FOLD_ROWS, x.shape[0], FOLD_ROWS):
        acc = pair(acc, x[r:r + FOLD_ROWS])
    return reduce(acc, axis=0, keepdims=True)


def _col_max(s):
    return _col_fold(s, jnp.maximum, jnp.max)


def _softmax_pc_t(s, m, ct):
    p = jnp.exp2(s - m)
    ot = _dot(ct, p.astype(BF16))
    return (ot[:KV_LORA] / ot[KV_LORA:KV_LORA + 1]).T


def _attn_kernel(*refs, ctx_queries):
    qn_refs, q0_refs, refs = refs[:TPS], refs[TPS:2 * TPS], refs[2 * TPS:]
    if ctx_queries:
        qc_ref, k_ref, ct_ref, o_ref, octx_ref, *scratch = refs
    else:
        k_ref, ct_ref, o_ref, *scratch = refs
    sa, sb = scratch[:TPS], scratch[TPS:2 * TPS]
    ma, mb = scratch[2 * TPS:3 * TPS], scratch[3 * TPS:]
    b, h, i = pl.program_id(0), pl.program_id(1), pl.program_id(2)

    def scores_stage(q_ref, s_ref, m_ref):
        s = _dot_t(k_ref[0], q_ref[0, 0])
        s_ref[...] = s
        m_ref[...] = _col_max(s)

    def output_stage(t, s_ref, m_ref):
        o_ref[0, t * TQ:(t + 1) * TQ, :] = _softmax_pc_t(
            s_ref[...], m_ref[...], ct_ref[0]).astype(BF16)

    def step(s_in, m_in, s_out, m_out):
        for t in range(TPS):
            scores_stage(qn_refs[t], s_out[t], m_out[t])
            output_stage(t, s_in[t], m_in[t])

    @pl.when(jnp.logical_and(jnp.logical_and(b == 0, h == 0), i == 0))
    def _():
        for t in range(TPS):
            scores_stage(q0_refs[t], sa[t], ma[t])

    if ctx_queries:
        @pl.when(i == 0)
        def _():
            sc = _dot_t(k_ref[0, :CTX_LEN], qc_ref[0, 0])
            octx_ref[0] = _softmax_pc_t(sc, _col_max(sc), ct_ref[0, :, :CTX_LEN]).astype(BF16)

    @pl.when(i % 2 == 0)
    def _():
        step(sa, ma, sb, mb)

    @pl.when(i % 2 == 1)
    def _():
        step(sb, mb, sa, ma)


def _next_step(b, h, i):
    f = jnp.minimum((b * N_HEADS + h) * NQS + i + 1, BATCH * N_HEADS * NQS - 1)
    return f // (N_HEADS * NQS), (f // NQS) % N_HEADS, f % NQS


def _attn_call(q, k, ct, ctx_queries):
    def next_q(t):
        def index(b, h, i):
            nb, nh, ng = _next_step(b, h, i)
            return (nb, nh, ng * TPS + t + LAT0, 0)
        return index

    q_tile = (1, 1, TQ, HEAD_W)
    in_specs = [pl.BlockSpec(q_tile, next_q(t)) for t in range(TPS)]
    in_specs += [pl.BlockSpec(q_tile, functools.partial(lambda t, b, h, i: (0, 0, LAT0 + t, 0), t))
                 for t in range(TPS)]
    args = [q] * (2 * TPS)
    out_specs = [pl.BlockSpec((1, TPS * TQ, KV_LORA), lambda b, h, i: (b, i, h))]
    out_shape = [jax.ShapeDtypeStruct((BATCH, SEQ, O_W), BF16)]
    if ctx_queries:
        in_specs.append(pl.BlockSpec(q_tile, lambda b, h, i: (b, h, 0, 0)))
        args.append(q)
        out_specs.append(pl.BlockSpec((1, CTX_LEN, KV_LORA), lambda b, h, i: (b, 0, h)))
        out_shape.append(jax.ShapeDtypeStruct((BATCH, CTX_LEN, O_W), BF16))
    in_specs += [
        pl.BlockSpec((1, TS, HEAD_W), lambda b, h, i: (_next_step(b, h, i)[0], 0, 0)),
        pl.BlockSpec((1, CT_ROWS, TS), lambda b, h, i: (b, 0, 0)),
    ]
    return pl.pallas_call(
        functools.partial(_attn_kernel, ctx_queries=ctx_queries),
        grid=(BATCH, N_HEADS, NQS),
        in_specs=in_specs,
        out_specs=out_specs,
        out_shape=out_shape,
        scratch_shapes=([pltpu.VMEM((TS, TQ), F32)] * (2 * TPS)
                        + [pltpu.VMEM((1, TQ), F32)] * (2 * TPS)),
        compiler_params=_params(3),
        name="attention",
    )(*args, k, ct)


def _sigmoid(z):
    return 0.5 * (jnp.tanh(0.5 * z) + 1.0)


def _mix_kernel(*refs, n_x, n_t, tile_off, with_ctx):
    n_tile = n_x + 5 + int(with_ctx)
    (gpre_ref, wg_ref, weff_ref, wpool_ref, pscale_ref, wopool_ref, wout_ref, gpost_ref,
     out_ref, *ext_refs) = refs[MIX_TILES * n_tile:]
    for t in range(MIX_TILES):
        x_refs = refs[t * n_tile:t * n_tile + n_x]
        mod_ref, o_ref, *rest = refs[t * n_tile + n_x:(t + 1) * n_tile]
        if with_ctx:
            octx_ref, *rest = rest
        uc_ref, up_ref, un_ref = rest
        ext_ref = ext_refs[t]
        j = (pl.program_id(0) * MIX_TILES + t) % n_t + tile_off
        x = _load_rows(x_refs, j)
        m = mod_ref[0, 0]
        h = (_rms(x, gpre_ref[0]) * (1.0 + m[0:1]) + m[1:2]).astype(BF16)
        tg = jnp.tanh(_dot(h, wg_ref[0]))
        o = jnp.where(j == 0, octx_ref[0], o_ref[0]) if with_ctx else o_ref[0]
        att = _dot(o, weff_ref[0])

        is_ctx = j == 0
        first = jnp.logical_or(is_ctx, j == LAT0)
        last = jnp.logical_or(is_ctx, j == NT - 1)
        u = uc_ref[0]
        ext_ref[0:HALO] = jnp.where(first, 0.0, up_ref[0])
        ext_ref[HALO:HALO + TM] = u
        ext_ref[HALO + TM:] = jnp.where(last, 0.0, un_ref[0])
        seq_len = jnp.where(is_ctx, CTX_LEN, SEQ)
        pos = lax.broadcasted_iota(jnp.int32, (TM, 1), 0) + jnp.where(is_ctx, 0, j - LAT0) * TM
        ys = []
        for g, w in enumerate(POOL_WINDOWS):
            lanes = slice(g * POOL_GROUP, (g + 1) * POOL_GROUP)
            acc = ext_ref[HALO - w // 2:HALO - w // 2 + TM, lanes]
            for s in range(1 - w // 2, w - w // 2):
                acc = acc + ext_ref[HALO + s:HALO + s + TM, lanes]
            cnt = jnp.minimum(pos + (w - w // 2), seq_len) - jnp.maximum(pos - w // 2, 0)
            d = acc / cnt.astype(F32) - u[:, lanes]
            ys.append(_dot(d.astype(BF16), wpool_ref[0, g]))
        yp = jnp.concatenate(ys, axis=1) * pscale_ref[0]
        o_pool = _dot(yp.astype(BF16), wopool_ref[0])

        merged2 = (att + o_pool) + (tg[:, :D_MODEL] * att + tg[:, D_MODEL:] * o_pool)
        y = _dot(merged2.astype(BF16), wout_ref[0])
        out_ref[t * TM:(t + 1) * TM, :] = x + m[2:3] * _rms(y, gpost_ref[0])


def _mix_call(l, x_args, mod_all, gpre, wg, o, o_ctx, weff, u, wpool, pscale, wopool, wout, gpost):
    with_ctx = o_ctx is not None
    n_t = NT if with_ctx else NT_LAT
    off = 0 if with_ctx else LAT0
    hb = TM // HALO
    assert (BATCH * n_t) % MIX_TILES == 0

    def tile_specs(t):
        def at(fn):
            def index(p):
                f = p * MIX_TILES + t
                return fn(f // n_t, f % n_t + off)
            return index

        lat = lambda b, j: (b, jnp.maximum(j - LAT0, 0), 0)
        specs = ([pl.BlockSpec((1, TM, D_MODEL), at(lambda b, j: (b, 0, 0))),
                  pl.BlockSpec((1, TM, D_MODEL), at(lat))] if len(x_args) == 2
                 else [pl.BlockSpec((1, TM, D_MODEL), at(lambda b, j: (b, j, 0)))])
        specs.append(pl.BlockSpec((1, 1, N_MOD, D_MODEL),
                                  at(lambda b, j: (l, jnp.where(j == 0, BATCH, b), 0, 0))))
        specs.append(pl.BlockSpec((1, TM, O_W), at(lat)))
        if with_ctx:
            specs.append(pl.BlockSpec((1, CTX_LEN, O_W), at(lambda b, j: (b, 0, 0))))
        specs += [
            pl.BlockSpec((1, TM, POOL_W), at(lambda b, j: (b, j, 0))),
            pl.BlockSpec((1, HALO, POOL_W), at(lambda b, j: (b, jnp.maximum(j * hb - 1, 0), 0))),
            pl.BlockSpec((1, HALO, POOL_W),
                         at(lambda b, j: (b, jnp.minimum((j + 1) * hb, TS // HALO - 1), 0))),
        ]
        return specs

    tile_args = [*x_args, mod_all, o, *([o_ctx] if with_ctx else []), u, u, u]
    weights = [gpre, wg, weff, wpool, pscale, wopool, wout, gpost]
    out = pl.pallas_call(
        functools.partial(_mix_kernel, n_x=len(x_args), n_t=n_t, tile_off=off, with_ctx=with_ctx),
        grid=(BATCH * n_t // MIX_TILES,),
        in_specs=([s for t in range(MIX_TILES) for s in tile_specs(t)]
                  + [_layer_spec(w, l) for w in weights]),
        out_specs=pl.BlockSpec((MIX_TILES * TM, D_MODEL), lambda p: (p, 0)),
        out_shape=jax.ShapeDtypeStruct((BATCH * n_t * TM, D_MODEL), F32),
        scratch_shapes=[pltpu.VMEM((TM + 2 * HALO, POOL_W), F32)] * MIX_TILES,
        compiler_params=_params(1),
        name="mix",
    )(*(tile_args * MIX_TILES), *weights)
    return out.reshape(BATCH, n_t * TM, D_MODEL)


def _ffn_kernel(x_ref, *refs):
    mod_refs = refs[:FFN_TILES]
    gpre_ref, wgate_ref, wup_ref, wdown_ref, gpost_ref, out_ref = refs[FFN_TILES:]
    x = x_ref[...]
    rows = [slice(t * TM, (t + 1) * TM) for t in range(FFN_TILES)]
    mods = [ref[0, 0] for ref in mod_refs]
    h = jnp.concatenate(
        [_rms(x[r], gpre_ref[0]) * (1.0 + m[3:4]) + m[4:5] for r, m in zip(rows, mods)],
        axis=0).astype(BF16)
    a = _dot(h, wgate_ref[0])
    t = (a * _sigmoid(a)) * _dot(h, wup_ref[0])
    y = _rms(_dot(t.astype(BF16), wdown_ref[0]), gpost_ref[0])
    for r, m in zip(rows, mods):
        out_ref[r, :] = x[r] + m[5:6] * y[r]


def _ffn_call(l, xs, mod_all, gpre, wgate, wup, wdown, gpost, with_ctx):
    n_t = NT if with_ctx else NT_LAT
    off = 0 if with_ctx else LAT0
    n_rows = BATCH * n_t * TM
    assert (BATCH * n_t) % FFN_TILES == 0

    def mod_spec(t):
        def index(p):
            f = p * FFN_TILES + t
            return (l, jnp.where(f % n_t + off == 0, BATCH, f // n_t), 0, 0)
        return pl.BlockSpec((1, 1, N_MOD, D_MODEL), index)

    weights = [gpre, wgate, wup, wdown, gpost]
    rows = pl.BlockSpec((FFN_TILES * TM, D_MODEL), lambda p: (p, 0))
    out = pl.pallas_call(
        _ffn_kernel,
        grid=(BATCH * n_t // FFN_TILES,),
        in_specs=([rows] + [mod_spec(t) for t in range(FFN_TILES)]
                  + [_layer_spec(w, l) for w in weights]),
        out_specs=rows,
        out_shape=jax.ShapeDtypeStruct((n_rows, D_MODEL), F32),
        compiler_params=_params(1),
        name="ffn",
    )(xs.reshape(n_rows, D_MODEL), *([mod_all] * FFN_TILES), *weights)
    return out.reshape(BATCH, n_t * TM, D_MODEL)


def _rope_tables():
    rows = SEQ // GRID_W
    pos = np.stack([np.repeat(np.arange(rows), GRID_W), np.tile(np.arange(GRID_W), rows)], axis=-1)
    expo = (-np.arange(0, ROPE_AXIS, 2, dtype=np.float32) / np.float32(ROPE_AXIS)).astype(np.float64)
    ang = pos[:, :, None].astype(np.float64) * (ROPE_THETA ** expo)
    cos = np.stack([np.cos(ang), np.cos(ang)], axis=2).reshape(SEQ, QK_ROPE)
    sin = np.stack([-np.sin(ang), np.sin(ang)], axis=2).reshape(SEQ, QK_ROPE)
    cos = np.concatenate([np.ones((CTX_LEN, QK_ROPE)), cos], axis=0)
    sin = np.concatenate([np.zeros((CTX_LEN, QK_ROPE)), sin], axis=0)
    pad = np.zeros((TS, QK_ROPE))
    return (np.concatenate([cos, pad], axis=1).astype(np.float32),
            np.concatenate([sin, pad], axis=1).astype(np.float32))


def _rot_last(a):
    qt = QK_ROPE // 4
    return jnp.concatenate(
        [a[..., qt:2 * qt], a[..., :qt], a[..., 3 * qt:], a[..., 2 * qt:3 * qt]], axis=-1)


def kernel(x, c, ctx, c_ctx, w_ada, b_ada, g_pre_mix, w_in, q_norm, w_uq, kv_norm, w_ukv, w_o_mla,
           w_pool, pool_scale, w_o_pool, w_out, g_post_mix, g_pre_ffn, w_ffn_gate, w_ffn_up,
           w_ffn_down, g_post_ffn):
    assert x.shape == (BATCH, SEQ, D_MODEL) and ctx.shape == (BATCH, CTX_LEN, D_MODEL)
    cvec = jnp.concatenate(
        [c, c_ctx[None, :], jnp.zeros((MOD_ROWS - BATCH - 1, D_MODEL), F32)], axis=0)
    mod_all = _ada_call(cvec, w_ada, b_ada).reshape(DEPTH, MOD_ROWS, N_MOD, D_MODEL)
    cos_t, sin_t = _rope_tables()

    win_a, wg = _win_call(w_in)
    wq = w_uq.reshape(DEPTH, Q_LORA, N_HEADS, QK_NOPE + QK_ROPE)
    wqn = wq[..., :QK_NOPE].reshape(DEPTH, Q_LORA, N_HEADS * QK_NOPE)
    wq_rope = wq[..., QK_NOPE:]
    wqr = jnp.concatenate([wq_rope, _rot_last(wq_rope)], axis=-1).reshape(
        DEPTH, Q_LORA, N_HEADS * 2 * QK_ROPE)
    wqf, weff = _fold_call(wqn, wqr, w_ukv, w_o_mla)
    wpool, wopool = w_pool.astype(BF16), w_o_pool.astype(BF16)
    wout = (0.5 * w_out).astype(BF16)
    wgate, wup, wdown = w_ffn_gate.astype(BF16), w_ffn_up.astype(BF16), w_ffn_down.astype(BF16)
    vec = lambda a: a[:, None, :]
    gpre, gpost, qn, kvn = vec(g_pre_mix), vec(g_post_mix), vec(q_norm), vec(kv_norm)
    gpre_f, gpost_f, pscale = vec(g_pre_ffn), vec(g_post_ffn), vec(pool_scale)

    x_args = (ctx, x)
    for l in range(DEPTH):
        last = l == DEPTH - 1
        q, k, ct, u = _inproj_call(l, x_args, mod_all, gpre, win_a, qn, wqf, kvn, cos_t, sin_t)
        if last:
            (o,), o_ctx = _attn_call(q, k, ct, ctx_queries=False), None
        else:
            o, o_ctx = _attn_call(q, k, ct, ctx_queries=True)
        xs1 = _mix_call(l, x_args, mod_all, gpre, wg, o, o_ctx, weff, u, wpool, pscale, wopool,
                        wout, gpost)
        xs = _ffn_call(l, xs1, mod_all, gpre_f, wgate, wup, wdown, gpost_f, with_ctx=not last)
        x_args = (xs,)
    return xs
```
